```python
import math
import jax, jax.numpy as jnp
from jax import lax
import numpy as np

D_MODEL = 1024
BATCH = 32
SEQ = 2048
DEPTH = 1

HEAD_DIM = 64
N_HEADS = D_MODEL // HEAD_DIM
N_HEADS_FOX = N_HEADS // 2
N_HEADS_DIL = N_HEADS - N_HEADS_FOX
W_FOX = N_HEADS_FOX * HEAD_DIM
W_DIL = N_HEADS_DIL * HEAD_DIM
DILATION_PAIRS = ((128, 1), (512, 4), (2048, 16))
ROPE_THETA = 500000.0
ROPE_DIM = HEAD_DIM // 4
Q_BLOCK = 128
D_FF = -(-8 * D_MODEL // (3 * 256)) * 256
EPS = 1e-6
NEG = -1e30
IN_SPLITS = (W_FOX, 2 * W_FOX, 3 * W_FOX, 3 * W_FOX + N_HEADS_FOX,
             3 * W_FOX + N_HEADS_FOX + W_DIL, 3 * W_FOX + N_HEADS_FOX + 2 * W_DIL)
IN_COLS = 3 * W_FOX + N_HEADS_FOX + 3 * W_DIL

kernel_name = "hymba_fox_dilated_hybrid"


def rms_norm(x, g):
    xf = x.astype(jnp.float32)
    y = xf * lax.rsqrt(jnp.mean(xf * xf, axis=-1, keepdims=True) + EPS)
    return (y * g.astype(jnp.float32)).astype(x.dtype)


def partial_rope(x, pos):
    half = ROPE_DIM // 2
    inv_freq = jnp.power(jnp.float32(ROPE_THETA),
                         -jnp.arange(half, dtype=jnp.float32) * 2.0 / ROPE_DIM)
    ang = pos.astype(jnp.float32)[:, None] * inv_freq[None, :]
    cos = jnp.cos(ang)[None, :, None, :]
    sin = jnp.sin(ang)[None, :, None, :]
    x1 = x[..., :half]
    x2 = x[..., half:ROPE_DIM]
    return jnp.concatenate([x1 * cos - x2 * sin, x2 * cos + x1 * sin, x[..., ROPE_DIM:]], axis=-1)


def fox_attention(q, k, v, log_f):
    S = q.shape[1]
    c = jnp.transpose(jnp.cumsum(log_f, axis=1), (0, 2, 1))
    scale = HEAD_DIM ** -0.5
    outs = []
    for i in range(S // Q_BLOCK):
        q0, q1 = i * Q_BLOCK, (i + 1) * Q_BLOCK
        s = jnp.einsum('bqhe,bkhe->bhqk', q[:, q0:q1], k[:, :q1]) * scale
        s = s + (c[:, :, q0:q1, None] - c[:, :, None, :q1])
        mask = np.arange(q0, q1)[:, None] >= np.arange(q1)[None, :]
        s = jnp.where(mask[None, None], s, NEG)
        p = jax.nn.softmax(s, axis=-1)
        outs.append(jnp.einsum('bhqk,bkhe->bqhe', p, v[:, :q1]))
    return jnp.concatenate(outs, axis=1)


def dilated_branch(q, k, v, window, dilation):
    B, S, H, D = q.shape
    n = S // dilation
    wk = window // dilation
    pad = min(wk, n)
    bq = math.gcd(Q_BLOCK, n)
    nb = n // bq
    L = bq + pad
    scale = HEAD_DIM ** -0.5
    qs = q.reshape(B, nb, bq, dilation, H, D)
    kp = jnp.pad(k.reshape(B, n, dilation, H, D), ((0, 0), (pad, 0), (0, 0), (0, 0), (0, 0)))
    vp = jnp.pad(v.reshape(B, n, dilation, H, D), ((0, 0), (pad, 0), (0, 0), (0, 0), (0, 0)))
    starts = np.arange(nb) * bq
    idx = starts[:, None] + np.arange(L)[None, :]
    kb = kp[:, idx]
    vb = vp[:, idx]
    s = jnp.einsum('bnirhe,bnjrhe->bnrhij', qs, kb) * scale
    ii = np.arange(bq)[:, None]
    jj = np.arange(L)[None, :]
    dist = ii + pad - jj
    key_pos = starts[:, None, None] + jj[None] - pad
    mask = (dist >= 0)[None] & (dist <= wk)[None] & (key_pos >= 0)
    s = jnp.where(mask[None, :, None, None], s, NEG)
    lse = jax.nn.logsumexp(s, axis=-1)
    p = jnp.exp(s - lse[..., None])
    o = jnp.einsum('bnrhij,bnjrhe->bnirhe', p, vb).reshape(B, S, H, D)
    lse = jnp.transpose(lse, (0, 1, 4, 2, 3)).reshape(B, S, H)
    return o, lse


def dilated_attention(q, k, v):
    outs, lses = [], []
    for window, dilation in DILATION_PAIRS:
        o, l = dilated_branch(q, k, v, window, dilation)
        outs.append(o)
        lses.append(l)
    w = jax.nn.softmax(jnp.stack(lses, axis=0), axis=0)
    return jnp.sum(w[..., None] * jnp.stack(outs, axis=0), axis=0)


def setup_inputs(seed: int = 0) -> dict:
    key = jax.random.key(seed)
    ks = jax.random.split(key, 16)
    f32 = jnp.float32

    def gain(k, shape):
        return jnp.ones(shape, f32) + 0.02 * jax.random.normal(k, shape, f32)

    return {
        "x": jax.random.normal(ks[0], (BATCH, SEQ, D_MODEL), f32),
        "g_mix": gain(ks[1], (DEPTH, D_MODEL)),
        "w_in": jax.random.normal(ks[2], (DEPTH, D_MODEL, IN_COLS), f32) * D_MODEL ** -0.5,
        "b_forget": jax.random.uniform(ks[3], (DEPTH, N_HEADS_FOX), f32, minval=1.0, maxval=4.0),
        "g_q_fox": gain(ks[4], (DEPTH, HEAD_DIM)),
        "g_k_fox": gain(ks[5], (DEPTH, HEAD_DIM)),
        "g_q_dil": gain(ks[6], (DEPTH, HEAD_DIM)),
        "g_k_dil": gain(ks[7], (DEPTH, HEAD_DIM)),
        "g_out_fox": gain(ks[8], (DEPTH, W_FOX)),
        "g_out_dil": gain(ks[9], (DEPTH, W_DIL)),
        "w_out": jax.random.normal(ks[10], (DEPTH, D_MODEL, D_MODEL), f32) * D_MODEL ** -0.5,
        "g_ffn": gain(ks[11], (DEPTH, D_MODEL)),
        "w_gate": jax.random.normal(ks[12], (DEPTH, D_MODEL, D_FF), f32) * D_MODEL ** -0.5,
        "w_up": jax.random.normal(ks[13], (DEPTH, D_MODEL, D_FF), f32) * D_MODEL ** -0.5,
        "w_down": jax.random.normal(ks[14], (DEPTH, D_FF, D_MODEL), f32) * D_FF ** -0.5,
    }


def reference(x, g_mix, w_in, b_forget, g_q_fox, g_k_fox, g_q_dil, g_k_dil,
              g_out_fox, g_out_dil, w_out, g_ffn, w_gate, w_up, w_down):
    B, S, _ = x.shape
    f32 = jnp.float32
    pos = jnp.arange(S)

    def heads(t, n_h):
        return t.reshape(B, S, n_h, HEAD_DIM).astype(f32)

    for l in range(DEPTH):
        h = rms_norm(x, g_mix[l])
        proj = jnp.einsum('bsd,dc->bsc', h, w_in[l])
        qa, ka, va, fa, qd, kd, vd = jnp.split(proj, IN_SPLITS, axis=-1)

        qa = rms_norm(heads(qa, N_HEADS_FOX), g_q_fox[l])
        ka = rms_norm(heads(ka, N_HEADS_FOX), g_k_fox[l])
        va = heads(va, N_HEADS_FOX)
        log_f = jax.nn.log_sigmoid(fa.astype(f32) + b_forget[l].astype(f32))
        o_fox = fox_attention(qa, ka, va, log_f).reshape(B, S, W_FOX)

        qd = partial_rope(rms_norm(heads(qd, N_HEADS_DIL), g_q_dil[l]), pos)
        kd = partial_rope(rms_norm(heads(kd, N_HEADS_DIL), g_k_dil[l]), pos)
        vd = heads(vd, N_HEADS_DIL)
        o_dil = dilated_attention(qd, kd, vd).reshape(B, S, W_DIL)

        o = jnp.concatenate([rms_norm(o_fox, g_out_fox[l]), rms_norm(o_dil, g_out_dil[l])], axis=-1)
        x = x + jnp.einsum('bsc,cd->bsd', o.astype(x.dtype), w_out[l])

        h = rms_norm(x, g_ffn[l])
        a = jnp.einsum('bsd,df->bsf', h, w_gate[l])
        u = jnp.einsum('bsd,df->bsf', h, w_up[l])
        x = x + jnp.einsum('bsf,fd->bsd', jax.nn.silu(a) * u, w_down[l])
    return x
```

```python
import functools
import math

import jax
import jax.numpy as jnp
import numpy as np
from jax import lax
from jax.experimental import pallas as pl
from jax.experimental.pallas import tpu as pltpu

F32 = jnp.float32
BF16 = jnp.bfloat16

HEAD_DIM = 64
ROPE_DIM = HEAD_DIM // 4
ROPE_THETA = 500000.0
DILATION_PAIRS = ((128, 1), (512, 4), (2048, 16))
EPS = 1e-6
NEG = -1e30

LANES = 128
MXU_DIM = 256
HEADS_PER_BLOCK = LANES // HEAD_DIM
AUG_TERMS = 3
AUG_STRIDE = 8
ONES_LANE = 3 * AUG_STRIDE

TM_PROJ = 512
TQ = 256
TM_FFN = 512
VMEM_LIMIT = 56 * 1024 * 1024


def _split3(v):
    hi = v.astype(BF16).astype(F32)
    r1 = v - hi
    mid = r1.astype(BF16).astype(F32)
    lo = (r1 - mid).astype(BF16).astype(F32)
    return hi, mid, lo


def _pack3(v, lane):
    hi, mid, lo = _split3(v)
    packed = hi + pltpu.roll(mid, AUG_STRIDE, 1) + pltpu.roll(lo, 2 * AUG_STRIDE, 1)
    return packed + jnp.where(lane == ONES_LANE, 1.0, 0.0)


def _proj_kernel(x_ref, gmix_ref, wmain_ref, wfa_ref, bf_ref, gains_ref, bd_ref, ltri_ref,
                 pq_ref, pk_ref, cos_ref, s1_ref, s2_ref,
                 qqf_ref, kkf_ref, vf_ref, qd_ref, kd_ref, vd_ref, carry_ref, *, n_heads):
    t = pl.program_id(1)
    tm = x_ref.shape[0]
    w = n_heads * HEAD_DIM
    n_blk = w // LANES

    x = x_ref[...]
    ms = jnp.mean(x * x, axis=-1, keepdims=True)
    h = (x * lax.rsqrt(ms + EPS) * gmix_ref[...]).astype(BF16)

    def group(i):
        return jnp.dot(h, wmain_ref[:, i * w:(i + 1) * w], preferred_element_type=F32)

    def head_norm(p, gain):
        sq = (p * p).astype(BF16)
        parts = [jnp.dot(sq[:, j * MXU_DIM:(j + 1) * MXU_DIM], bd_ref[...],
                         preferred_element_type=F32) for j in range(w // MXU_DIM)]
        msq = jnp.concatenate(parts, axis=1)
        return p * lax.rsqrt(msq + EPS) * gain

    def rope(xn):
        c, s1, s2 = cos_ref[...], s1_ref[...], s2_ref[...]
        outs = []
        for j in range(n_blk):
            xc = xn[:, j * LANES:(j + 1) * LANES]
            outs.append(xc * c + pltpu.roll(xc, LANES - ROPE_DIM // 2, 1) * s1
                        + pltpu.roll(xc, ROPE_DIM // 2, 1) * s2)
        return outs

    lane = lax.broadcasted_iota(jnp.int32, (tm, LANES), 1)
    fa = jnp.dot(h, wfa_ref[...], preferred_element_type=F32) + bf_ref[...]
    logf = jnp.minimum(fa, 0.0) - jnp.log1p(jnp.exp(-jnp.abs(fa)))
    logf = jnp.where(lane < n_heads, logf, 0.0)
    packed = _pack3(logf, lane).astype(BF16)
    cpart = jnp.dot(ltri_ref[...], packed, preferred_element_type=F32)
    ctile = (cpart + pltpu.roll(cpart, LANES - AUG_STRIDE, 1)
             + pltpu.roll(cpart, LANES - 2 * AUG_STRIDE, 1))

    @pl.when(t == 0)
    def _():
        carry_ref[...] = jnp.zeros_like(carry_ref)

    c = jnp.where(lane < n_heads, ctile + carry_ref[0:1, :], 0.0)
    carry_ref[...] = jnp.broadcast_to(c[tm - 1:tm, :], carry_ref.shape)
    cpk = _pack3(c, lane).astype(BF16)
    qa = jnp.dot(cpk, pq_ref[...], preferred_element_type=F32).astype(BF16)
    ka = jnp.dot(cpk, pk_ref[...], preferred_element_type=F32).astype(BF16)

    qf = head_norm(group(0), gains_ref[0:1, :])
    kf = head_norm(group(1), gains_ref[1:2, :])
    for j in range(n_blk):
        qqf_ref[:, 2 * j * LANES:(2 * j + 1) * LANES] = qf[:, j * LANES:(j + 1) * LANES].astype(BF16)
        qqf_ref[:, (2 * j + 1) * LANES:(2 * j + 2) * LANES] = qa[:, j * LANES:(j + 1) * LANES]
        kkf_ref[:, 2 * j * LANES:(2 * j + 1) * LANES] = kf[:, j * LANES:(j + 1) * LANES].astype(BF16)
        kkf_ref[:, (2 * j + 1) * LANES:(2 * j + 2) * LANES] = ka[:, j * LANES:(j + 1) * LANES]
    vf_ref[...] = group(2).astype(BF16)

    qd = rope(head_norm(group(3), gains_ref[2:3, :]))
    kd = rope(head_norm(group(4), gains_ref[3:4, :]))
    for j in range(n_blk):
        qd_ref[:, j * LANES:(j + 1) * LANES] = qd[j].astype(BF16)
        kd_ref[:, j * LANES:(j + 1) * LANES] = kd[j].astype(BF16)
    vd_ref[...] = group(5).astype(BF16)


def _attn_kernel(qq_ref, kk_ref, v_ref, g_ref, o_ref, *, dense_bias):
    s_len, kw = qq_ref.shape
    lane_q = lax.broadcasted_iota(jnp.int32, (TQ, kw), 1)
    lane_v = lax.broadcasted_iota(jnp.int32, (s_len, LANES), 1)
    lane_o = lax.broadcasted_iota(jnp.int32, (TQ, LANES), 1)

    q_masks, v_heads = [], []
    for j in range(HEADS_PER_BLOCK):
        m = (lane_q >= j * HEAD_DIM) & (lane_q < (j + 1) * HEAD_DIM)
        if kw > LANES:
            a0 = LANES + j * AUG_STRIDE
            m = m | ((lane_q >= a0) & (lane_q < a0 + 2 * AUG_TERMS))
        q_masks.append(m)
        own = (lane_v >= j * HEAD_DIM) & (lane_v < (j + 1) * HEAD_DIM)
        v_heads.append(jnp.where(own, v_ref[...], jnp.ones((), BF16)))

    for i in range(s_len // TQ):
        q0, q1 = i * TQ, (i + 1) * TQ
        qb = qq_ref[q0:q1, :]
        kb = kk_ref[0:q1, :]
        res = []
        for j in range(HEADS_PER_BLOCK):
            qm = jnp.where(q_masks[j], qb, jnp.zeros((), BF16))
            s = lax.dot_general(qm, kb, (((1,), (1,)), ((), ())), preferred_element_type=F32)
            if dense_bias:
                s = s + g_ref[:, s_len - q1:s_len]
                m = jnp.max(s, axis=-1, keepdims=True)
                p = jnp.exp(s - m).astype(BF16)
            else:
                sd = s[:, q0:q1] + g_ref[:, s_len - TQ:s_len]
                m = jnp.max(sd, axis=-1, keepdims=True)
                if i > 0:
                    so = s[:, 0:q0]
                    m = jnp.maximum(m, jnp.max(so, axis=-1, keepdims=True))
                    p = jnp.concatenate([jnp.exp(so - m).astype(BF16),
                                         jnp.exp(sd - m).astype(BF16)], axis=1)
                else:
                    p = jnp.exp(sd - m).astype(BF16)
            oe = jnp.dot(p, v_heads[j][0:q1, :], preferred_element_type=F32)
            res.append(oe / pltpu.roll(oe, HEAD_DIM, 1))
        o_ref[q0:q1, :] = jnp.where(lane_o < HEAD_DIM, res[0], res[1]).astype(o_ref.dtype)


def _ffn_kernel(x_ref, of_ref, od_ref, gof_ref, god_ref, wof_ref, wod_ref, gffn_ref,
                wg_ref, wu_ref, wd_ref, out_ref):
    def norm(v, g):
        return v * lax.rsqrt(jnp.mean(v * v, axis=-1, keepdims=True) + EPS) * g

    nf = norm(of_ref[...].astype(F32), gof_ref[...]).astype(BF16)
    nd = norm(od_ref[...].astype(F32), god_ref[...]).astype(BF16)
    x1 = (x_ref[...] + jnp.dot(nf, wof_ref[...], preferred_element_type=F32)
          + jnp.dot(nd, wod_ref[...], preferred_element_type=F32))
    h = norm(x1, gffn_ref[...]).astype(BF16)
    a = jnp.dot(h, wg_ref[...], preferred_element_type=F32)
    u = jnp.dot(h, wu_ref[...], preferred_element_type=F32)
    g = (a * (1.0 / (1.0 + jnp.exp(-a))) * u).astype(BF16)
    out_ref[...] = x1 + jnp.dot(g, wd_ref[...], preferred_element_type=F32)


def _rope_tables(s_len):
    half = ROPE_DIM // 2
    inv_freq = jnp.power(jnp.float32(ROPE_THETA),
                         -jnp.arange(half, dtype=jnp.float32) * 2.0 / ROPE_DIM)
    ang = jnp.arange(s_len).astype(jnp.float32)[:, None] * inv_freq[None, :]
    cos, sin = jnp.cos(ang), jnp.sin(ang)
    ones = jnp.ones((s_len, HEAD_DIM - ROPE_DIM), F32)
    zeros = jnp.zeros((s_len, HEAD_DIM - ROPE_DIM), F32)
    zh = jnp.zeros((s_len, half), F32)
    c_head = jnp.concatenate([cos, cos, ones], axis=1)
    s1_head = jnp.concatenate([-sin, zh, zeros], axis=1)
    s2_head = jnp.concatenate([zh, sin, zeros], axis=1)
    rep = lambda a: jnp.tile(a, (1, HEADS_PER_BLOCK))
    return rep(c_head), rep(s1_head), rep(s2_head)


def _aug_selectors(n_heads):
    w = n_heads * HEAD_DIM
    pq = np.zeros((LANES, w), np.float32)
    pk = np.zeros((LANES, w), np.float32)
    for hd in range(n_heads):
        base = (hd // HEADS_PER_BLOCK) * LANES + (hd % HEADS_PER_BLOCK) * AUG_STRIDE
        for k in range(AUG_TERMS):
            pq[AUG_STRIDE * k + hd, base + k] = 1.0
            pq[ONES_LANE, base + AUG_TERMS + k] = 1.0
            pk[ONES_LANE, base + k] = 1.0
            pk[AUG_STRIDE * k + hd, base + AUG_TERMS + k] = -1.0
    return pq, pk


def _bias_tables(s_len):
    i = np.arange(TQ)[:, None]
    jp = np.arange(s_len)[None, :]
    delta = i - jp + s_len - TQ
    causal = np.where(delta >= 0, 0.0, NEG).astype(np.float32)
    mult = np.zeros(delta.shape, np.float64)
    for window, dil in DILATION_PAIRS:
        mult += (delta >= 0) & (delta <= window) & (delta % dil == 0)
    dil_bias = np.where(mult > 0, np.log(np.maximum(mult, 1.0)), NEG).astype(np.float32)
    return causal, dil_bias


def _resident(shape):
    nd = len(shape)
    return pl.BlockSpec(shape, lambda *_: (0,) * nd, pipeline_mode=pl.Buffered(1))


def _layer(x, g_mix, w_in, b_forget, g_q_fox, g_k_fox, g_q_dil, g_k_dil,
           g_out_fox, g_out_dil, w_out, g_ffn, w_gate, w_up, w_down):
    b_sz, s_len, d = x.shape
    n_heads = b_forget.shape[0]
    w = n_heads * HEAD_DIM
    d_ff = w_gate.shape[1]
    assert w_in.shape[1] == 6 * w + n_heads and w % MXU_DIM == 0
    assert s_len % TM_PROJ == 0 and s_len % TQ == 0 and (b_sz * s_len) % TM_FFN == 0
    assert d % LANES == 0 and n_heads <= AUG_STRIDE

    cols = np.cumsum([0, w, w, w, n_heads, w, w, w])
    w_main = jnp.concatenate([w_in[:, cols[i]:cols[i + 1]] for i in (0, 1, 2, 4, 5, 6)],
                             axis=1).astype(BF16)
    w_fa = jnp.zeros((d, LANES), F32).at[:, :n_heads].set(w_in[:, cols[3]:cols[4]]).astype(BF16)
    b_pad = jnp.zeros((1, LANES), F32).at[0, :n_heads].set(b_forget)
    scale = HEAD_DIM ** -0.5
    gains = jnp.stack([jnp.tile(g_q_fox * scale, n_heads), jnp.tile(g_k_fox, n_heads),
                       jnp.tile(g_q_dil * scale, n_heads), jnp.tile(g_k_dil, n_heads)])
    head_id = np.arange(MXU_DIM) // HEAD_DIM
    bd = jnp.asarray((head_id[:, None] == head_id[None, :]) / HEAD_DIM, BF16)
    ltri = jnp.asarray(np.tril(np.ones((TM_PROJ, TM_PROJ), np.float32)), BF16)
    pq, pk = _aug_selectors(n_heads)
    cos_t, s1_t, s2_t = _rope_tables(s_len)
    causal, dil_bias = _bias_tables(s_len)

    tok = lambda width: pl.BlockSpec((None, TM_PROJ, width), lambda b, t: (b, t, 0))
    pos = pl.BlockSpec((TM_PROJ, LANES), lambda b, t: (t, 0))
    bf = lambda width: jax.ShapeDtypeStruct((b_sz, s_len, width), BF16)
    qqf, kkf, vf, qd, kd, vd = pl.pallas_call(
        functools.partial(_proj_kernel, n_heads=n_heads),
        grid=(b_sz, s_len // TM_PROJ),
        in_specs=[tok(d), _resident((1, d)), _resident((d, 6 * w)), _resident((d, LANES)),
                  _resident((1, LANES)), _resident((4, w)), _resident((MXU_DIM, MXU_DIM)),
                  _resident((TM_PROJ, TM_PROJ)), _resident((LANES, w)), _resident((LANES, w)),
                  pos, pos, pos],
        out_specs=[tok(2 * w), tok(2 * w), tok(w), tok(w), tok(w), tok(w)],
        out_shape=[bf(2 * w), bf(2 * w), bf(w), bf(w), bf(w), bf(w)],
        scratch_shapes=[pltpu.VMEM((8, LANES), F32)],
        compiler_params=pltpu.CompilerParams(
            dimension_semantics=("arbitrary", "arbitrary"), vmem_limit_bytes=VMEM_LIMIT),
        name="proj",
    )(x, g_mix.reshape(1, d), w_main, w_fa, b_pad, gains, bd, ltri,
      jnp.asarray(pq, BF16), jnp.asarray(pk, BF16), cos_t, s1_t, s2_t)

    def attention(qq, kk, v, bias, dense_bias, name):
        kw = qq.shape[-1] // (w // LANES)
        blk = lambda width: pl.BlockSpec((None, s_len, width), lambda b, p: (b, 0, p))
        return pl.pallas_call(
            functools.partial(_attn_kernel, dense_bias=dense_bias),
            grid=(b_sz, w // LANES),
            in_specs=[blk(kw), blk(kw), blk(LANES), _resident((TQ, s_len))],
            out_specs=blk(LANES),
            out_shape=bf(w),
            compiler_params=pltpu.CompilerParams(
                dimension_semantics=("arbitrary", "arbitrary"), vmem_limit_bytes=VMEM_LIMIT),
            name=name,
        )(qq, kk, v, jnp.asarray(bias))

    o_fox = attention(qqf, kkf, vf, causal, False, "attn_fox")
    o_dil = attention(qd, kd, vd, dil_bias, True, "attn_dil")

    n_tok = b_sz * s_len
    rows = lambda width: pl.BlockSpec((TM_FFN, width), lambda t: (t, 0))
    out = pl.pallas_call(
        _ffn_kernel,
        grid=(n_tok // TM_FFN,),
        in_specs=[rows(d), rows(w), rows(w), _resident((1, w)), _resident((1, w)),
                  _resident((w, d)), _resident((w, d)), _resident((1, d)),
                  _resident((d, d_ff)), _resident((d, d_ff)), _resident((d_ff, d))],
        out_specs=rows(d),
        out_shape=jax.ShapeDtypeStruct((n_tok, d), x.dtype),
        compiler_params=pltpu.CompilerParams(
            dimension_semantics=("arbitrary",), vmem_limit_bytes=VMEM_LIMIT),
        name="ffn",
    )(x.reshape(n_tok, d), o_fox.reshape(n_tok, w), o_dil.reshape(n_tok, w),
      g_out_fox.reshape(1, w), g_out_dil.reshape(1, w),
      w_out[:w].astype(BF16), w_out[w:].astype(BF16), g_ffn.reshape(1, d),
      w_gate.astype(BF16), w_up.astype(BF16), w_down.astype(BF16))
    return out.reshape(b_sz, s_len, d)


def kernel(x, g_mix, w_in, b_forget, g_q_fox, g_k_fox, g_q_dil, g_k_dil, g_out_fox, g_out_dil,
           w_out, g_ffn, w_gate, w_up, w_down):
    for l in range(g_mix.shape[0]):
        x = _layer(x, g_mix[l], w_in[l], b_forget[l], g_q_fox[l], g_k_fox[l], g_q_dil[l],
                   g_k_dil[l], g_out_fox[l], g_out_dil[l], w_out[l], g_ffn[l], w_gate[l],
                   w_up[l], w_down[l])
    return x
```

```python
import functools
import math

import jax
import jax.numpy as jnp
import numpy as np
from jax import lax
from jax.experimental import pallas as pl
from jax.experimental.pallas import tpu as pltpu

F32 = jnp.float32
BF16 = jnp.bfloat16

HEAD_DIM = 64
ROPE_DIM = HEAD_DIM // 4
ROPE_THETA = 500000.0
DILATION_PAIRS = ((128, 1), (512, 4), (2048, 16))
EPS = 1e-6
NEG = -1e30
LOG2E = math.log2(math.e)

LANES = 128
MXU_DIM = 256
HEADS_PER_BLOCK = LANES // HEAD_DIM
ONES_ROWS = 16
AUG_TERMS = 3
AUG_STRIDE = 8
ONES_LANE = 3 * AUG_STRIDE

TM_PROJ = 512
TQ = 256
TM_FFN = 512
VMEM_LIMIT = 56 * 1024 * 1024


def _split3(v):
    hi = v.astype(BF16).astype(F32)
    r1 = v - hi
    mid = r1.astype(BF16).astype(F32)
    lo = (r1 - mid).astype(BF16).astype(F32)
    return hi, mid, lo


def _pack3(v, lane):
    hi, mid, lo = _split3(v)
    packed = hi + pltpu.roll(mid, AUG_STRIDE, 1) + pltpu.roll(lo, 2 * AUG_STRIDE, 1)
    return packed + jnp.where(lane == ONES_LANE, 1.0, 0.0)


def _proj_kernel(x_ref, gmix_ref, wmain_ref, wfa_ref, bf_ref, gains_ref, bd_ref, ltri_ref,
                 pq_ref, pk_ref, cos_ref, s1_ref, s2_ref,
                 qqf_ref, kkf_ref, vf_ref, qd_ref, kd_ref, vd_ref, carry_ref, *, n_heads):
    t = pl.program_id(1)
    tm = x_ref.shape[0]
    w = n_heads * HEAD_DIM
    n_blk = w // LANES

    x = x_ref[...]
    ms = jnp.mean(x * x, axis=-1, keepdims=True)
    h = (x * lax.rsqrt(ms + EPS) * gmix_ref[...]).astype(BF16)

    def group(i):
        return jnp.dot(h, wmain_ref[:, i * w:(i + 1) * w], preferred_element_type=F32)

    def head_norm(p, gain):
        sq = (p * p).astype(BF16)
        parts = [jnp.dot(sq[:, j * MXU_DIM:(j + 1) * MXU_DIM], bd_ref[...],
                         preferred_element_type=F32) for j in range(w // MXU_DIM)]
        msq = jnp.concatenate(parts, axis=1)
        return p * lax.rsqrt(msq + EPS) * gain

    def rope(xn):
        c, s1, s2 = cos_ref[...], s1_ref[...], s2_ref[...]
        outs = []
        for j in range(n_blk):
            xc = xn[:, j * LANES:(j + 1) * LANES]
            outs.append(xc * c + pltpu.roll(xc, LANES - ROPE_DIM // 2, 1) * s1
                        + pltpu.roll(xc, ROPE_DIM // 2, 1) * s2)
        return outs

    lane = lax.broadcasted_iota(jnp.int32, (tm, LANES), 1)
    fa = jnp.dot(h, wfa_ref[...], preferred_element_type=F32) + bf_ref[...]
    logf = jnp.minimum(fa, 0.0) - jnp.log1p(jnp.exp(-jnp.abs(fa)))
    logf = jnp.where(lane < n_heads, logf * LOG2E, 0.0)
    packed = _pack3(logf, lane).astype(BF16)
    cpart = jnp.dot(ltri_ref[...], packed, preferred_element_type=F32)
    ctile = (cpart + pltpu.roll(cpart, LANES - AUG_STRIDE, 1)
             + pltpu.roll(cpart, LANES - 2 * AUG_STRIDE, 1))

    @pl.when(t == 0)
    def _():
        carry_ref[...] = jnp.zeros_like(carry_ref)

    c = jnp.where(lane < n_heads, ctile + carry_ref[0:1, :], 0.0)
    carry_ref[...] = jnp.broadcast_to(c[tm - 1:tm, :], carry_ref.shape)
    cpk = _pack3(c, lane).astype(BF16)
    qa = jnp.dot(cpk, pq_ref[...], preferred_element_type=F32).astype(BF16)
    ka = jnp.dot(cpk, pk_ref[...], preferred_element_type=F32).astype(BF16)

    qf = head_norm(group(0), gains_ref[0:1, :])
    kf = head_norm(group(1), gains_ref[1:2, :])
    for j in range(n_blk):
        qqf_ref[:, 2 * j * LANES:(2 * j + 1) * LANES] = qf[:, j * LANES:(j + 1) * LANES].astype(BF16)
        qqf_ref[:, (2 * j + 1) * LANES:(2 * j + 2) * LANES] = qa[:, j * LANES:(j + 1) * LANES]
        kkf_ref[:, 2 * j * LANES:(2 * j + 1) * LANES] = kf[:, j * LANES:(j + 1) * LANES].astype(BF16)
        kkf_ref[:, (2 * j + 1) * LANES:(2 * j + 2) * LANES] = ka[:, j * LANES:(j + 1) * LANES]
    vf_ref[...] = group(2).astype(BF16)

    qd = rope(head_norm(group(3), gains_ref[2:3, :]))
    kd = rope(head_norm(group(4), gains_ref[3:4, :]))
    for j in range(n_blk):
        qd_ref[:, j * LANES:(j + 1) * LANES] = qd[j].astype(BF16)
        kd_ref[:, j * LANES:(j + 1) * LANES] = kd[j].astype(BF16)
    vd_ref[...] = group(5).astype(BF16)


def _attn_kernel(qq_ref, kk_ref, v_ref, g_ref, o_ref, vt_scr, s_scr0, s_scr1, s_scr2,
                 acc_scr, *,
                 dense_bias):
    s_len, kw = qq_ref.shape
    n_units = s_len // TQ
    lane_q = lax.broadcasted_iota(jnp.int32, (TQ, kw), 1)
    vt_scr[0:LANES, :] = v_ref[...].T
    vt_scr[LANES:, :] = jnp.ones((ONES_ROWS, s_len), BF16)

    q_masks = []
    for j in range(HEADS_PER_BLOCK):
        m = (lane_q >= j * HEAD_DIM) & (lane_q < (j + 1) * HEAD_DIM)
        if kw > LANES:
            a0 = LANES + j * AUG_STRIDE
            m = m | ((lane_q >= a0) & (lane_q < a0 + 2 * AUG_TERMS))
        q_masks.append(m)

    s_scr = (s_scr0, s_scr1, s_scr2)
    col_max = [None] * n_units

    def scores(i, c, qcat):
        s = lax.dot_general(kk_ref[c * TQ:(c + 1) * TQ, :], qcat, (((1,), (1,)), ((), ())),
                            preferred_element_type=F32)
        if dense_bias or c == i:
            r0 = s_len - (i + 1 - c) * TQ
            g = g_ref[r0:r0 + TQ, :]
            s = s + jnp.concatenate([g] * HEADS_PER_BLOCK, axis=1)
        s_scr[i % len(s_scr)][c * TQ:(c + 1) * TQ, :] = s
        cm = jnp.max(s, axis=0, keepdims=True)
        col_max[i] = cm if col_max[i] is None else jnp.maximum(col_max[i], cm)

    def probs(i, c):
        return jnp.exp2((s_scr[i % len(s_scr)][c * TQ:(c + 1) * TQ, :] - col_max[i]).astype(BF16))

    def weighted_values(c, p, first):
        o = jnp.dot(vt_scr[:, c * TQ:(c + 1) * TQ], p, preferred_element_type=F32)
        acc_scr[...] = o if first else acc_scr[...] + o

    def score_steps(i):
        qb = qq_ref[i * TQ:(i + 1) * TQ, :]
        qcat = jnp.concatenate([jnp.where(q_masks[j], qb, jnp.zeros((), BF16))
                                for j in range(HEADS_PER_BLOCK)], axis=0)
        return [functools.partial(scores, i, c, qcat) for c in range(i + 1)]

    def finish(i):
        res = []
        for j in range(HEADS_PER_BLOCK):
            num = acc_scr[j * HEAD_DIM:(j + 1) * HEAD_DIM, j * TQ:(j + 1) * TQ]
            den = acc_scr[LANES:LANES + 1, j * TQ:(j + 1) * TQ]
            res.append(num / den)
        o_ref[i * TQ:(i + 1) * TQ, :] = jnp.concatenate(res, axis=0).T.astype(o_ref.dtype)
        col_max[i] = None

    lead = len(s_scr) - 1
    for u in range(min(lead, n_units)):
        for step in score_steps(u):
            step()
    for i in range(n_units):
        n_chunks = i + 1
        nxt = score_steps(i + lead) if i + lead < n_units else []
        p_prev = None
        for k in range(max(n_chunks + 1, len(nxt))):
            if k < len(nxt):
                nxt[k]()
            if p_prev is not None:
                weighted_values(k - 1, p_prev, k == 1)
            p_prev = probs(i, k) if k < n_chunks else None
        finish(i)


def _ffn_kernel(x_ref, of_ref, od_ref, gof_ref, god_ref, wof_ref, wod_ref, gffn_ref,
                wg_ref, wu_ref, wd_ref, out_ref):
    def norm(v, g):
        return v * lax.rsqrt(jnp.mean(v * v, axis=-1, keepdims=True) + EPS) * g

    nf = norm(of_ref[...].astype(F32), gof_ref[...]).astype(BF16)
    nd = norm(od_ref[...].astype(F32), god_ref[...]).astype(BF16)
    x1 = (x_ref[...] + jnp.dot(nf, wof_ref[...], preferred_element_type=F32)
          + jnp.dot(nd, wod_ref[...], preferred_element_type=F32))
    h = norm(x1, gffn_ref[...]).astype(BF16)
    a = jnp.dot(h, wg_ref[...], preferred_element_type=F32)
    u = jnp.dot(h, wu_ref[...], preferred_element_type=F32)
    g = (a * (1.0 / (1.0 + jnp.exp(-a))) * u).astype(BF16)
    out_ref[...] = x1 + jnp.dot(g, wd_ref[...], preferred_element_type=F32)


def _rope_tables(s_len):
    half = ROPE_DIM // 2
    inv_freq = jnp.power(jnp.float32(ROPE_THETA),
                         -jnp.arange(half, dtype=jnp.float32) * 2.0 / ROPE_DIM)
    ang = jnp.arange(s_len).astype(jnp.float32)[:, None] * inv_freq[None, :]
    cos, sin = jnp.cos(ang), jnp.sin(ang)
    ones = jnp.ones((s_len, HEAD_DIM - ROPE_DIM), F32)
    zeros = jnp.zeros((s_len, HEAD_DIM - ROPE_DIM), F32)
    zh = jnp.zeros((s_len, half), F32)
    c_head = jnp.concatenate([cos, cos, ones], axis=1)
    s1_head = jnp.concatenate([-sin, zh, zeros], axis=1)
    s2_head = jnp.concatenate([zh, sin, zeros], axis=1)
    rep = lambda a: jnp.tile(a, (1, HEADS_PER_BLOCK))
    return rep(c_head), rep(s1_head), rep(s2_head)


def _aug_selectors(n_heads):
    w = n_heads * HEAD_DIM
    pq = np.zeros((LANES, w), np.float32)
    pk = np.zeros((LANES, w), np.float32)
    for hd in range(n_heads):
        base = (hd // HEADS_PER_BLOCK) * LANES + (hd % HEADS_PER_BLOCK) * AUG_STRIDE
        for k in range(AUG_TERMS):
            pq[AUG_STRIDE * k + hd, base + k] = 1.0
            pq[ONES_LANE, base + AUG_TERMS + k] = 1.0
            pk[ONES_LANE, base + k] = 1.0
            pk[AUG_STRIDE * k + hd, base + AUG_TERMS + k] = -1.0
    return pq, pk


def _bias_tables(s_len):
    i = np.arange(TQ)[None, :]
    jp = np.arange(s_len)[:, None]
    delta = i - jp + s_len - TQ
    causal = np.where(delta >= 0, 0.0, NEG).astype(np.float32)
    mult = np.zeros(delta.shape, np.float64)
    for window, dil in DILATION_PAIRS:
        mult += (delta >= 0) & (delta <= window) & (delta % dil == 0)
    dil_bias = np.where(mult > 0, np.log2(np.maximum(mult, 1.0)), NEG).astype(np.float32)
    return causal, dil_bias


def _resident(shape):
    nd = len(shape)
    return pl.BlockSpec(shape, lambda *_: (0,) * nd, pipeline_mode=pl.Buffered(1))


def _layer(x, g_mix, w_in, b_forget, g_q_fox, g_k_fox, g_q_dil, g_k_dil,
           g_out_fox, g_out_dil, w_out, g_ffn, w_gate, w_up, w_down):
    b_sz, s_len, d = x.shape
    n_heads = b_forget.shape[0]
    w = n_heads * HEAD_DIM
    d_ff = w_gate.shape[1]
    assert w_in.shape[1] == 6 * w + n_heads and w % MXU_DIM == 0
    assert s_len % TM_PROJ == 0 and s_len % TQ == 0 and (b_sz * s_len) % TM_FFN == 0
    assert d % LANES == 0 and n_heads <= AUG_STRIDE

    cols = np.cumsum([0, w, w, w, n_heads, w, w, w])
    w_main = jnp.concatenate([w_in[:, cols[i]:cols[i + 1]] for i in (0, 1, 2, 4, 5, 6)],
                             axis=1).astype(BF16)
    w_fa = jnp.zeros((d, LANES), F32).at[:, :n_heads].set(w_in[:, cols[3]:cols[4]]).astype(BF16)
    b_pad = jnp.zeros((1, LANES), F32).at[0, :n_heads].set(b_forget)
    scale = HEAD_DIM ** -0.5 * LOG2E
    gains = jnp.stack([jnp.tile(g_q_fox * scale, n_heads), jnp.tile(g_k_fox, n_heads),
                       jnp.tile(g_q_dil * scale, n_heads), jnp.tile(g_k_dil, n_heads)])
    head_id = np.arange(MXU_DIM) // HEAD_DIM
    bd = jnp.asarray((head_id[:, None] == head_id[None, :]) / HEAD_DIM, BF16)
    ltri = jnp.asarray(np.tril(np.ones((TM_PROJ, TM_PROJ), np.float32)), BF16)
    pq, pk = _aug_selectors(n_heads)
    cos_t, s1_t, s2_t = _rope_tables(s_len)
    causal, dil_bias = _bias_tables(s_len)

    tok = lambda width: pl.BlockSpec((None, TM_PROJ, width), lambda b, t: (b, t, 0))
    pos = pl.BlockSpec((TM_PROJ, LANES), lambda b, t: (t, 0))
    bf = lambda width: jax.ShapeDtypeStruct((b_sz, s_len, width), BF16)
    qqf, kkf, vf, qd, kd, vd = pl.pallas_call(
        functools.partial(_proj_kernel, n_heads=n_heads),
        grid=(b_sz, s_len // TM_PROJ),
        in_specs=[tok(d), _resident((1, d)), _resident((d, 6 * w)), _resident((d, LANES)),
                  _resident((1, LANES)), _resident((4, w)), _resident((MXU_DIM, MXU_DIM)),
                  _resident((TM_PROJ, TM_PROJ)), _resident((LANES, w)), _resident((LANES, w)),
                  pos, pos, pos],
        out_specs=[tok(2 * w), tok(2 * w), tok(w), tok(w), tok(w), tok(w)],
        out_shape=[bf(2 * w), bf(2 * w), bf(w), bf(w), bf(w), bf(w)],
        scratch_shapes=[pltpu.VMEM((8, LANES), F32)],
        compiler_params=pltpu.CompilerParams(
            dimension_semantics=("arbitrary", "arbitrary"), vmem_limit_bytes=VMEM_LIMIT),
        name="proj",
    )(x, g_mix.reshape(1, d), w_main, w_fa, b_pad, gains, bd, ltri,
      jnp.asarray(pq, BF16), jnp.asarray(pk, BF16), cos_t, s1_t, s2_t)

    def attention(qq, kk, v, bias, dense_bias, name):
        kw = qq.shape[-1] // (w // LANES)
        blk = lambda width: pl.BlockSpec((None, s_len, width), lambda b, p: (b, 0, p))
        return pl.pallas_call(
            functools.partial(_attn_kernel, dense_bias=dense_bias),
            grid=(b_sz, w // LANES),
            in_specs=[blk(kw), blk(kw), blk(LANES), _resident((s_len, TQ))],
            out_specs=blk(LANES),
            out_shape=bf(w),
            scratch_shapes=[pltpu.VMEM((LANES + ONES_ROWS, s_len), BF16),
                            pltpu.VMEM((s_len, HEADS_PER_BLOCK * TQ), F32),
                            pltpu.VMEM((s_len, HEADS_PER_BLOCK * TQ), F32),
                            pltpu.VMEM((s_len, HEADS_PER_BLOCK * TQ), F32),
                            pltpu.VMEM((LANES + ONES_ROWS, HEADS_PER_BLOCK * TQ), F32)],
            compiler_params=pltpu.CompilerParams(
                dimension_semantics=("arbitrary", "arbitrary"), vmem_limit_bytes=VMEM_LIMIT),
            name=name,
        )(qq, kk, v, jnp.asarray(bias))

    o_fox = attention(qqf, kkf, vf, causal, False, "attn_fox")
    o_dil = attention(qd, kd, vd, dil_bias, True, "attn_dil")

    n_tok = b_sz * s_len
    rows = lambda width: pl.BlockSpec((TM_FFN, width), lambda t: (t, 0))
    out = pl.pallas_call(
        _ffn_kernel,
        grid=(n_tok // TM_FFN,),
        in_specs=[rows(d), rows(w), rows(w), _resident((1, w)), _resident((1, w)),
                  _resident((w, d)), _resident((w, d)), _resident((1, d)),
                  _resident((d, d_ff)), _resident((d, d_ff)), _resident((d_ff, d))],
        out_specs=rows(d),
        out_shape=jax.ShapeDtypeStruct((n_tok, d), x.dtype),
        compiler_params=pltpu.CompilerParams(
            dimension_semantics=("arbitrary",), vmem_limit_bytes=VMEM_LIMIT),
        name="ffn",
    )(x.reshape(n_tok, d), o_fox.reshape(n_tok, w), o_dil.reshape(n_tok, w),
      g_out_fox.reshape(1, w), g_out_dil.reshape(1, w),
      w_out[:w].astype(BF16), w_out[w:].astype(BF16), g_ffn.reshape(1, d),
      w_gate.astype(BF16), w_up.astype(BF16), w_down.astype(BF16))
    return out.reshape(b_sz, s_len, d)


def kernel(x, g_mix, w_in, b_forget, g_q_fox, g_k_fox, g_q_dil, g_k_dil, g_out_fox, g_out_dil,
           w_out, g_ffn, w_gate, w_up, w_down):
    for l in range(g_mix.shape[0]):
        x = _layer(x, g_mix[l], w_in[l], b_forget[l], g_q_fox[l], g_k_fox[l], g_q_dil[l],
                   g_k_dil[l], g_out_fox[l], g_out_dil[l], w_out[l], g_ffn[l], w_gate[l],
                   w_up[l], w_down[l])
    return x
```

```python
import functools
import math

import jax
import jax.numpy as jnp
import numpy as np
from jax import lax
from jax.experimental import pallas as pl
from jax.experimental.pallas import tpu as pltpu

F32 = jnp.float32
BF16 = jnp.bfloat16

HEAD_DIM = 64
ROPE_DIM = HEAD_DIM // 4
ROPE_THETA = 500000.0
DILATION_PAIRS = ((128, 1), (512, 4), (2048, 16))
EPS = 1e-6
NEG = -1e30
LOG2E = math.log2(math.e)

LANES = 128
MXU_DIM = 256
HEADS_PER_BLOCK = LANES // HEAD_DIM
ONES_ROWS = 16
AUG_TERMS = 3
AUG_STRIDE = 8
ONES_LANE = 3 * AUG_STRIDE

TM_PROJ = 512
TQ = 256
TM_FFN = 512
VMEM_LIMIT = 56 * 1024 * 1024
PV_LAG = 1
EXP2_RANGE = 120.0
BOUND_SLACK = 1.02


def _split3(v):
    hi = v.astype(BF16).astype(F32)
    r1 = v - hi
    mid = r1.astype(BF16).astype(F32)
    lo = (r1 - mid).astype(BF16).astype(F32)
    return hi, mid, lo


def _pack3(v, lane):
    hi, mid, lo = _split3(v)
    packed = hi + pltpu.roll(mid, AUG_STRIDE, 1) + pltpu.roll(lo, 2 * AUG_STRIDE, 1)
    return packed + jnp.where(lane == ONES_LANE, 1.0, 0.0)


def _proj_kernel(x_ref, gmix_ref, wmain_ref, wfa_ref, bf_ref, bnd_ref, gains_ref, bd_ref, ltri_ref,
                 pq_ref, pk_ref, cos_ref, s1_ref, s2_ref,
                 qqf_ref, kkf_ref, vf_ref, qd_ref, kd_ref, vd_ref, carry_ref, *, n_heads):
    t = pl.program_id(1)
    tm = x_ref.shape[0]
    w = n_heads * HEAD_DIM
    n_blk = w // LANES

    x = x_ref[...]
    ms = jnp.mean(x * x, axis=-1, keepdims=True)
    h = (x * lax.rsqrt(ms + EPS) * gmix_ref[...]).astype(BF16)

    def group(i):
        return jnp.dot(h, wmain_ref[:, i * w:(i + 1) * w], preferred_element_type=F32)

    def head_norm(p, gain):
        sq = (p * p).astype(BF16)
        parts = [jnp.dot(sq[:, j * MXU_DIM:(j + 1) * MXU_DIM], bd_ref[...],
                         preferred_element_type=F32) for j in range(w // MXU_DIM)]
        msq = jnp.concatenate(parts, axis=1)
        return p * lax.rsqrt(msq + EPS) * gain

    def rope(xn):
        c, s1, s2 = cos_ref[...], s1_ref[...], s2_ref[...]
        outs = []
        for j in range(n_blk):
            xc = xn[:, j * LANES:(j + 1) * LANES]
            outs.append(xc * c + pltpu.roll(xc, LANES - ROPE_DIM // 2, 1) * s1
                        + pltpu.roll(xc, ROPE_DIM // 2, 1) * s2)
        return outs

    lane = lax.broadcasted_iota(jnp.int32, (tm, LANES), 1)
    fa = jnp.dot(h, wfa_ref[...], preferred_element_type=F32) + bf_ref[...]
    logf = jnp.minimum(fa, 0.0) - jnp.log1p(jnp.exp(-jnp.abs(fa)))
    logf = jnp.where(lane < n_heads, logf * LOG2E, 0.0)
    packed = _pack3(logf, lane).astype(BF16)
    cpart = jnp.dot(ltri_ref[...], packed, preferred_element_type=F32)
    ctile = (cpart + pltpu.roll(cpart, LANES - AUG_STRIDE, 1)
             + pltpu.roll(cpart, LANES - 2 * AUG_STRIDE, 1))

    @pl.when(t == 0)
    def _():
        carry_ref[...] = jnp.zeros_like(carry_ref)

    c = jnp.where(lane < n_heads, ctile + carry_ref[0:1, :], 0.0)
    carry_ref[...] = jnp.broadcast_to(c[tm - 1:tm, :], carry_ref.shape)
    cq = _pack3(c, lane).astype(BF16)
    ck = _pack3(jnp.where(lane < n_heads, c + bnd_ref[...], 0.0), lane).astype(BF16)
    qa = jnp.dot(cq, pq_ref[...], preferred_element_type=F32).astype(BF16)
    ka = jnp.dot(ck, pk_ref[...], preferred_element_type=F32).astype(BF16)

    qf = head_norm(group(0), gains_ref[0:1, :])
    kf = head_norm(group(1), gains_ref[1:2, :])
    for j in range(n_blk):
        qqf_ref[:, 2 * j * LANES:(2 * j + 1) * LANES] = qf[:, j * LANES:(j + 1) * LANES].astype(BF16)
        qqf_ref[:, (2 * j + 1) * LANES:(2 * j + 2) * LANES] = qa[:, j * LANES:(j + 1) * LANES]
        kkf_ref[:, 2 * j * LANES:(2 * j + 1) * LANES] = kf[:, j * LANES:(j + 1) * LANES].astype(BF16)
        kkf_ref[:, (2 * j + 1) * LANES:(2 * j + 2) * LANES] = ka[:, j * LANES:(j + 1) * LANES]
    vf_ref[...] = group(2).astype(BF16)

    qd = rope(head_norm(group(3), gains_ref[2:3, :]))
    kd = rope(head_norm(group(4), gains_ref[3:4, :]))
    for j in range(n_blk):
        qd_ref[:, j * LANES:(j + 1) * LANES] = qd[j].astype(BF16)
        kd_ref[:, j * LANES:(j + 1) * LANES] = kd[j].astype(BF16)
    vd_ref[...] = group(5).astype(BF16)


def _attn_exact_kernel(qq_ref, kk_ref, v_ref, g_ref, o_ref, vt_scr, s_scr0, s_scr1, s_scr2,
                       acc_scr, *, dense_bias):
    s_len, kw = qq_ref.shape
    n_units = s_len // TQ
    lane_q = lax.broadcasted_iota(jnp.int32, (TQ, kw), 1)
    vt_scr[0:LANES, :] = v_ref[...].T
    vt_scr[LANES:, :] = jnp.ones((ONES_ROWS, s_len), BF16)

    q_masks = []
    for j in range(HEADS_PER_BLOCK):
        m = (lane_q >= j * HEAD_DIM) & (lane_q < (j + 1) * HEAD_DIM)
        if kw > LANES:
            a0 = LANES + j * AUG_STRIDE
            m = m | ((lane_q >= a0) & (lane_q < a0 + 2 * AUG_TERMS))
        q_masks.append(m)

    s_scr = (s_scr0, s_scr1, s_scr2)
    col_max = [None] * n_units

    def scores(i, c, qcat):
        s = lax.dot_general(kk_ref[c * TQ:(c + 1) * TQ, :], qcat, (((1,), (1,)), ((), ())),
                            preferred_element_type=F32)
        if dense_bias or c == i:
            r0 = s_len - (i + 1 - c) * TQ
            g = g_ref[r0:r0 + TQ, :]
            s = s + jnp.concatenate([g] * HEADS_PER_BLOCK, axis=1)
        s_scr[i % len(s_scr)][c * TQ:(c + 1) * TQ, :] = s
        cm = jnp.max(s, axis=0, keepdims=True)
        col_max[i] = cm if col_max[i] is None else jnp.maximum(col_max[i], cm)

    def probs(i, c):
        return jnp.exp2((s_scr[i % len(s_scr)][c * TQ:(c + 1) * TQ, :] - col_max[i]).astype(BF16))

    def weighted_values(c, p, first):
        o = jnp.dot(vt_scr[:, c * TQ:(c + 1) * TQ], p, preferred_element_type=F32)
        acc_scr[...] = o if first else acc_scr[...] + o

    def score_steps(i):
        qb = qq_ref[i * TQ:(i + 1) * TQ, :]
        qcat = jnp.concatenate([jnp.where(q_masks[j], qb, jnp.zeros((), BF16))
                                for j in range(HEADS_PER_BLOCK)], axis=0)
        return [functools.partial(scores, i, c, qcat) for c in range(i + 1)]

    def finish(i):
        res = []
        for j in range(HEADS_PER_BLOCK):
            num = acc_scr[j * HEAD_DIM:(j + 1) * HEAD_DIM, j * TQ:(j + 1) * TQ]
            den = acc_scr[LANES:LANES + 1, j * TQ:(j + 1) * TQ]
            res.append(num / den)
        o_ref[i * TQ:(i + 1) * TQ, :] = jnp.concatenate(res, axis=0).T.astype(o_ref.dtype)
        col_max[i] = None

    lead = len(s_scr) - 1
    for u in range(min(lead, n_units)):
        for step in score_steps(u):
            step()
    for i in range(n_units):
        n_chunks = i + 1
        nxt = score_steps(i + lead) if i + lead < n_units else []
        p_prev = None
        for k in range(max(n_chunks + 1, len(nxt))):
            if k < len(nxt):
                nxt[k]()
            if p_prev is not None:
                weighted_values(k - 1, p_prev, k == 1)
            p_prev = probs(i, k) if k < n_chunks else None
        finish(i)


def _attn_bounded_kernel(qq_ref, kk_ref, v_ref, g_ref, o_ref, vt_scr, p_scr0, p_scr1, *,
                         dense_bias):
    s_len, kw = qq_ref.shape
    n_units = s_len // TQ
    lane_q = lax.broadcasted_iota(jnp.int32, (TQ, kw), 1)
    vt_scr[0:LANES, :] = v_ref[...].T
    vt_scr[LANES:, :] = jnp.ones((ONES_ROWS, s_len), BF16)

    q_masks = []
    for j in range(HEADS_PER_BLOCK):
        m = (lane_q >= j * HEAD_DIM) & (lane_q < (j + 1) * HEAD_DIM)
        if kw > LANES:
            a0 = LANES + j * AUG_STRIDE
            m = m | ((lane_q >= a0) & (lane_q < a0 + 2 * AUG_TERMS))
        q_masks.append(m)
    p_scr = (p_scr0, p_scr1)

    def chunk(i, c, qcat):
        s = lax.dot_general(kk_ref[c * TQ:(c + 1) * TQ, :], qcat, (((1,), (1,)), ((), ())),
                            preferred_element_type=F32)
        if dense_bias or c == i:
            r0 = s_len - (i + 1 - c) * TQ
            g = g_ref[r0:r0 + TQ, :]
            s = s + jnp.concatenate([g] * HEADS_PER_BLOCK, axis=1)
        p_scr[i % len(p_scr)][c * TQ:(c + 1) * TQ, :] = jnp.exp2(s).astype(BF16)

    def unit_steps(i):
        qb = qq_ref[i * TQ:(i + 1) * TQ, :]
        qcat = jnp.concatenate([jnp.where(q_masks[j], qb, jnp.zeros((), BF16))
                                for j in range(HEADS_PER_BLOCK)], axis=0)
        return [functools.partial(chunk, i, c, qcat) for c in range(i + 1)]

    def finish(i):
        q1 = (i + 1) * TQ
        o = jnp.dot(vt_scr[:, 0:q1], p_scr[i % len(p_scr)][0:q1, :],
                    preferred_element_type=F32)
        res = []
        for j in range(HEADS_PER_BLOCK):
            num = o[j * HEAD_DIM:(j + 1) * HEAD_DIM, j * TQ:(j + 1) * TQ]
            den = o[LANES:LANES + 1, j * TQ:(j + 1) * TQ]
            res.append(num / den)
        o_ref[i * TQ:(i + 1) * TQ, :] = jnp.concatenate(res, axis=0).T.astype(o_ref.dtype)

    for i in range(n_units):
        steps = unit_steps(i)
        pv_at = min(PV_LAG, len(steps) - 1)
        for k, step in enumerate(steps):
            step()
            if i > 0 and k == pv_at:
                finish(i - 1)
    finish(n_units - 1)


def _ffn_kernel(x_ref, of_ref, od_ref, gof_ref, god_ref, wof_ref, wod_ref, gffn_ref,
                wg_ref, wu_ref, wd_ref, out_ref):
    def norm(v, g):
        return v * lax.rsqrt(jnp.mean(v * v, axis=-1, keepdims=True) + EPS) * g

    nf = norm(of_ref[...].astype(F32), gof_ref[...]).astype(BF16)
    nd = norm(od_ref[...].astype(F32), god_ref[...]).astype(BF16)
    x1 = (x_ref[...] + jnp.dot(nf, wof_ref[...], preferred_element_type=F32)
          + jnp.dot(nd, wod_ref[...], preferred_element_type=F32))
    h = norm(x1, gffn_ref[...]).astype(BF16)
    a = jnp.dot(h, wg_ref[...], preferred_element_type=F32)
    u = jnp.dot(h, wu_ref[...], preferred_element_type=F32)
    g = (a * (1.0 / (1.0 + jnp.exp(-a))) * u).astype(BF16)
    out_ref[...] = x1 + jnp.dot(g, wd_ref[...], preferred_element_type=F32)


def _rope_tables(s_len):
    half = ROPE_DIM // 2
    inv_freq = jnp.power(jnp.float32(ROPE_THETA),
                         -jnp.arange(half, dtype=jnp.float32) * 2.0 / ROPE_DIM)
    ang = jnp.arange(s_len).astype(jnp.float32)[:, None] * inv_freq[None, :]
    cos, sin = jnp.cos(ang), jnp.sin(ang)
    ones = jnp.ones((s_len, HEAD_DIM - ROPE_DIM), F32)
    zeros = jnp.zeros((s_len, HEAD_DIM - ROPE_DIM), F32)
    zh = jnp.zeros((s_len, half), F32)
    c_head = jnp.concatenate([cos, cos, ones], axis=1)
    s1_head = jnp.concatenate([-sin, zh, zeros], axis=1)
    s2_head = jnp.concatenate([zh, sin, zeros], axis=1)
    rep = lambda a: jnp.tile(a, (1, HEADS_PER_BLOCK))
    return rep(c_head), rep(s1_head), rep(s2_head)


def _aug_selectors(n_heads):
    w = n_heads * HEAD_DIM
    pq = np.zeros((LANES, w), np.float32)
    pk = np.zeros((LANES, w), np.float32)
    for hd in range(n_heads):
        base = (hd // HEADS_PER_BLOCK) * LANES + (hd % HEADS_PER_BLOCK) * AUG_STRIDE
        for k in range(AUG_TERMS):
            pq[AUG_STRIDE * k + hd, base + k] = 1.0
            pq[ONES_LANE, base + AUG_TERMS + k] = 1.0
            pk[ONES_LANE, base + k] = 1.0
            pk[AUG_STRIDE * k + hd, base + AUG_TERMS + k] = -1.0
    return pq, pk


def _bias_tables(s_len):
    i = np.arange(TQ)[None, :]
    jp = np.arange(s_len)[:, None]
    delta = i - jp + s_len - TQ
    causal = np.where(delta >= 0, 0.0, NEG).astype(np.float32)
    mult = np.zeros(delta.shape, np.float64)
    for window, dil in DILATION_PAIRS:
        mult += (delta >= 0) & (delta <= window) & (delta % dil == 0)
    dil_bias = np.where(mult > 0, np.log2(np.maximum(mult, 1.0)), NEG).astype(np.float32)
    return causal, dil_bias, float(np.log2(len(DILATION_PAIRS)))


def _resident(shape):
    nd = len(shape)
    return pl.BlockSpec(shape, lambda *_: (0,) * nd, pipeline_mode=pl.Buffered(1))


def _layer(x, g_mix, w_in, b_forget, g_q_fox, g_k_fox, g_q_dil, g_k_dil,
           g_out_fox, g_out_dil, w_out, g_ffn, w_gate, w_up, w_down):
    b_sz, s_len, d = x.shape
    n_heads = b_forget.shape[0]
    w = n_heads * HEAD_DIM
    d_ff = w_gate.shape[1]
    assert w_in.shape[1] == 6 * w + n_heads and w % MXU_DIM == 0
    assert s_len % TM_PROJ == 0 and s_len % TQ == 0 and (b_sz * s_len) % TM_FFN == 0
    assert d % LANES == 0 and n_heads <= AUG_STRIDE

    cols = np.cumsum([0, w, w, w, n_heads, w, w, w])
    w_main = jnp.concatenate([w_in[:, cols[i]:cols[i + 1]] for i in (0, 1, 2, 4, 5, 6)],
                             axis=1).astype(BF16)
    w_fa = jnp.zeros((d, LANES), F32).at[:, :n_heads].set(w_in[:, cols[3]:cols[4]]).astype(BF16)
    b_pad = jnp.zeros((1, LANES), F32).at[0, :n_heads].set(b_forget)
    scale = HEAD_DIM ** -0.5 * LOG2E
    gains = jnp.stack([jnp.tile(g_q_fox * scale, n_heads), jnp.tile(g_k_fox, n_heads),
                       jnp.tile(g_q_dil * scale, n_heads), jnp.tile(g_k_dil, n_heads)])
    head_id = np.arange(MXU_DIM) // HEAD_DIM
    bd = jnp.asarray((head_id[:, None] == head_id[None, :]) / HEAD_DIM, BF16)
    ltri = jnp.asarray(np.tril(np.ones((TM_PROJ, TM_PROJ), np.float32)), BF16)
    pq, pk = _aug_selectors(n_heads)
    cos_t, s1_t, s2_t = _rope_tables(s_len)
    causal, dil_bias, dil_bias_max = _bias_tables(s_len)

    def logit_bound(gq, gk):
        return (HEAD_DIM * scale * BOUND_SLACK) * jnp.max(jnp.abs(gq)) * jnp.max(jnp.abs(gk))
    bound_fox = logit_bound(g_q_fox, g_k_fox)
    bound_dil = logit_bound(g_q_dil, g_k_dil)
    bounded_ok = 2.0 * jnp.maximum(bound_fox, bound_dil) + dil_bias_max <= EXP2_RANGE
    bnd_pad = jnp.zeros((1, LANES), F32).at[0, :n_heads].set(bound_fox)

    tok = lambda width: pl.BlockSpec((None, TM_PROJ, width), lambda b, t: (b, t, 0))
    pos = pl.BlockSpec((TM_PROJ, LANES), lambda b, t: (t, 0))
    bf = lambda width: jax.ShapeDtypeStruct((b_sz, s_len, width), BF16)
    qqf, kkf, vf, qd, kd, vd = pl.pallas_call(
        functools.partial(_proj_kernel, n_heads=n_heads),
        grid=(b_sz, s_len // TM_PROJ),
        in_specs=[tok(d), _resident((1, d)), _resident((d, 6 * w)), _resident((d, LANES)),
                  _resident((1, LANES)), _resident((1, LANES)), _resident((4, w)),
                  _resident((MXU_DIM, MXU_DIM)),
                  _resident((TM_PROJ, TM_PROJ)), _resident((LANES, w)), _resident((LANES, w)),
                  pos, pos, pos],
        out_specs=[tok(2 * w), tok(2 * w), tok(w), tok(w), tok(w), tok(w)],
        out_shape=[bf(2 * w), bf(2 * w), bf(w), bf(w), bf(w), bf(w)],
        scratch_shapes=[pltpu.VMEM((8, LANES), F32)],
        compiler_params=pltpu.CompilerParams(
            dimension_semantics=("arbitrary", "arbitrary"), vmem_limit_bytes=VMEM_LIMIT),
        name="proj",
    )(x, g_mix.reshape(1, d), w_main, w_fa, b_pad, bnd_pad, gains, bd, ltri,
      jnp.asarray(pq, BF16), jnp.asarray(pk, BF16), cos_t, s1_t, s2_t)

    def attention(body, scratch, qq, kk, v, bias, dense_bias, name):
        kw = qq.shape[-1] // (w // LANES)
        blk = lambda width: pl.BlockSpec((None, s_len, width), lambda b, p: (b, 0, p))
        return pl.pallas_call(
            functools.partial(body, dense_bias=dense_bias),
            grid=(b_sz, w // LANES),
            in_specs=[blk(kw), blk(kw), blk(LANES), _resident((s_len, TQ))],
            out_specs=blk(LANES),
            out_shape=bf(w),
            scratch_shapes=scratch,
            compiler_params=pltpu.CompilerParams(
                dimension_semantics=("arbitrary", "arbitrary"), vmem_limit_bytes=VMEM_LIMIT),
            name=name,
        )(qq, kk, v, bias)

    nq = HEADS_PER_BLOCK * TQ
    vt_shape = pltpu.VMEM((LANES + ONES_ROWS, s_len), BF16)
    exact_scratch = [vt_shape] + [pltpu.VMEM((s_len, nq), F32)] * 3 + [
        pltpu.VMEM((LANES + ONES_ROWS, nq), F32)]
    bounded_scratch = [vt_shape] + [pltpu.VMEM((s_len, nq), BF16)] * 2

    def attend_bounded():
        return (attention(_attn_bounded_kernel, bounded_scratch, qqf, kkf, vf,
                          jnp.asarray(causal), False, "attn_fox"),
                attention(_attn_bounded_kernel, bounded_scratch, qd, kd, vd,
                          jnp.asarray(dil_bias) - (bound_dil + dil_bias_max), True, "attn_dil"))

    def attend_exact():
        return (attention(_attn_exact_kernel, exact_scratch, qqf, kkf, vf,
                          jnp.asarray(causal), False, "attn_fox_exact"),
                attention(_attn_exact_kernel, exact_scratch, qd, kd, vd,
                          jnp.asarray(dil_bias), True, "attn_dil_exact"))

    o_fox, o_dil = lax.cond(bounded_ok, attend_bounded, attend_exact)

    n_tok = b_sz * s_len
    rows = lambda width: pl.BlockSpec((TM_FFN, width), lambda t: (t, 0))
    out = pl.pallas_call(
        _ffn_kernel,
        grid=(n_tok // TM_FFN,),
        in_specs=[rows(d), rows(w), rows(w), _resident((1, w)), _resident((1, w)),
                  _resident((w, d)), _resident((w, d)), _resident((1, d)),
                  _resident((d, d_ff)), _resident((d, d_ff)), _resident((d_ff, d))],
        out_specs=rows(d),
        out_shape=jax.ShapeDtypeStruct((n_tok, d), x.dtype),
        compiler_params=pltpu.CompilerParams(
            dimension_semantics=("arbitrary",), vmem_limit_bytes=VMEM_LIMIT),
        name="ffn",
    )(x.reshape(n_tok, d), o_fox.reshape(n_tok, w), o_dil.reshape(n_tok, w),
      g_out_fox.reshape(1, w), g_out_dil.reshape(1, w),
      w_out[:w].astype(BF16), w_out[w:].astype(BF16), g_ffn.reshape(1, d),
      w_gate.astype(BF16), w_up.astype(BF16), w_down.astype(BF16))
    return out.reshape(b_sz, s_len, d)


def kernel(x, g_mix, w_in, b_forget, g_q_fox, g_k_fox, g_q_dil, g_k_dil, g_out_fox, g_out_dil,
           w_out, g_ffn, w_gate, w_up, w_down):
    for l in range(g_mix.shape[0]):
        x = _layer(x, g_mix[l], w_in[l], b_forget[l], g_q_fox[l], g_k_fox[l], g_q_dil[l],
                   g_k_dil[l], g_out_fox[l], g_out_dil[l], w_out[l], g_ffn[l], w_gate[l],
                   w_up[l], w_down[l])
    return x
```

```python
import functools
import math

import jax
import jax.numpy as jnp
import numpy as np
from jax import lax
from jax.experimental import pallas as pl
from jax.experimental.pallas import tpu as pltpu

F32 = jnp.float32
BF16 = jnp.bfloat16

HEAD_DIM = 64
ROPE_DIM = HEAD_DIM // 4
ROPE_THETA = 500000.0
DILATION_PAIRS = ((128, 1), (512, 4), (2048, 16))
EPS = 1e-6
NEG = -1e30
LOG2E = math.log2(math.e)

LANES = 128
MXU_DIM = 256
HEADS_PER_BLOCK = LANES // HEAD_DIM
ONES_ROWS = 16
AUG_TERMS = 3
AUG_STRIDE = 8
Q_ONES = 4 * AUG_STRIDE

TM_PROJ = 512
TQ = 256
TM_FFN = 512
FFN_SPLIT = 2
VMEM_LIMIT = 56 * 1024 * 1024
PV_LAG = 1
EXP2_RANGE = 64.0
BOUND_SLACK = 1.02


def _split3(v):
    hi = v.astype(BF16).astype(F32)
    r1 = v - hi
    mid = r1.astype(BF16).astype(F32)
    lo = (r1 - mid).astype(BF16).astype(F32)
    return hi, mid, lo


def _pack3(v):
    hi, mid, lo = _split3(v)
    return hi + pltpu.roll(mid, AUG_STRIDE, 1) + pltpu.roll(lo, 2 * AUG_STRIDE, 1)


def _proj_kernel(x_ref, gmix_ref, wmain_ref, wfa_ref, bf_ref, bnd_ref, gains_ref, bd_ref,
                 cos_ref, s1_ref, s2_ref,
                 qqf_ref, kkf_ref, vf_ref, qd_ref, kd_ref, vd_ref, carry_ref, *, n_heads):
    t = pl.program_id(1)
    tm, d = x_ref.shape
    w = n_heads * HEAD_DIM
    n_blk = w // LANES

    x = x_ref[...]
    ms = jnp.mean(x * x, axis=-1, keepdims=True)
    h = (x * lax.rsqrt(ms + EPS) * gmix_ref[...]).astype(BF16)

    def group(i):
        return jnp.dot(h, wmain_ref[:, i * w:(i + 1) * w], preferred_element_type=F32)

    def mean_squares(p):
        sq = (p * p).astype(BF16)
        parts = [jnp.dot(sq[:, j * MXU_DIM:(j + 1) * MXU_DIM], bd_ref[...],
                         preferred_element_type=F32) for j in range(w // MXU_DIM)]
        return jnp.concatenate(parts, axis=1)

    def rope(xc):
        return (xc * cos_ref[...] + pltpu.roll(xc, LANES - ROPE_DIM // 2, 1) * s1_ref[...]
                + pltpu.roll(xc, ROPE_DIM // 2, 1) * s2_ref[...])

    lane = lax.broadcasted_iota(jnp.int32, (tm, LANES), 1)
    half = d // 2
    fa = (jnp.dot(h[:, :half], wfa_ref[:half, :], preferred_element_type=F32)
          + jnp.dot(h[:, half:], wfa_ref[half:, :], preferred_element_type=F32) + bf_ref[...])

    g_q = group(0)

    logf = jnp.minimum(fa, 0.0) - jnp.log1p(jnp.exp(-jnp.abs(fa)))
    logf = jnp.where(lane < n_heads, logf * LOG2E, 0.0)

    g_k = group(1)

    row = lax.broadcasted_iota(jnp.int32, (tm, LANES), 0)
    ctile = logf
    shift = 1
    while shift < tm:
        ctile = ctile + jnp.where(row >= shift, pltpu.roll(ctile, shift, 0), 0.0)
        shift *= 2

    @pl.when(t == 0)
    def _():
        carry_ref[...] = jnp.zeros_like(carry_ref)

    c = jnp.where(lane < n_heads, ctile + carry_ref[0:1, :], 0.0)
    carry_ref[...] = jnp.broadcast_to(c[tm - 1:tm, :], carry_ref.shape)
    cq = _pack3(c)
    ck = _pack3(jnp.where(lane < n_heads, c + bnd_ref[...], 0.0))
    term_lane = (lane % AUG_STRIDE < HEADS_PER_BLOCK) & (lane < AUG_TERMS * AUG_STRIDE)
    ones_lane = ((lane % AUG_STRIDE < HEADS_PER_BLOCK) & (lane >= Q_ONES)
                 & (lane < Q_ONES + AUG_TERMS * AUG_STRIDE))

    def gate_lanes(p):
        shift = HEADS_PER_BLOCK * p
        qa = jnp.where(term_lane, pltpu.roll(cq, (LANES - shift) % LANES, 1),
                       jnp.where(ones_lane, 1.0, 0.0))
        ka = jnp.where(ones_lane, -pltpu.roll(ck, (Q_ONES - shift) % LANES, 1),
                       jnp.where(term_lane, 1.0, 0.0))
        return qa.astype(BF16), ka.astype(BF16)

    ms_q = mean_squares(g_q)
    g_v = group(2)
    qf = g_q * lax.rsqrt(ms_q + EPS) * gains_ref[0:1, :]
    ms_k = mean_squares(g_k)
    vf_ref[...] = g_v.astype(BF16)
    g_qd = group(3)
    kf = g_k * lax.rsqrt(ms_k + EPS) * gains_ref[1:2, :]
    for j in range(n_blk):
        qa, ka = gate_lanes(j)
        qqf_ref[:, 2 * j * LANES:(2 * j + 1) * LANES] = qf[:, j * LANES:(j + 1) * LANES].astype(BF16)
        qqf_ref[:, (2 * j + 1) * LANES:(2 * j + 2) * LANES] = qa
        kkf_ref[:, 2 * j * LANES:(2 * j + 1) * LANES] = kf[:, j * LANES:(j + 1) * LANES].astype(BF16)
        kkf_ref[:, (2 * j + 1) * LANES:(2 * j + 2) * LANES] = ka

    ms_qd = mean_squares(g_qd)
    g_kd = group(4)
    qd = g_qd * lax.rsqrt(ms_qd + EPS) * gains_ref[2:3, :]
    ms_kd = mean_squares(g_kd)
    g_vd = group(5)
    kd = g_kd * lax.rsqrt(ms_kd + EPS) * gains_ref[3:4, :]
    for j in range(n_blk):
        qd_ref[:, j * LANES:(j + 1) * LANES] = rope(qd[:, j * LANES:(j + 1) * LANES]).astype(BF16)
        kd_ref[:, j * LANES:(j + 1) * LANES] = rope(kd[:, j * LANES:(j + 1) * LANES]).astype(BF16)
    vd_ref[...] = g_vd.astype(BF16)


def _attn_exact_kernel(qq_ref, kk_ref, v_ref, g_ref, o_ref, vt_scr, s_scr0, s_scr1, s_scr2,
                       acc_scr, *, dense_bias):
    s_len, kw = qq_ref.shape
    n_units = s_len // TQ
    lane_q = lax.broadcasted_iota(jnp.int32, (TQ, kw), 1)
    vt_scr[0:LANES, :] = v_ref[...].T
    vt_scr[LANES:, :] = jnp.ones((ONES_ROWS, s_len), BF16)

    q_masks = []
    for j in range(HEADS_PER_BLOCK):
        m = (lane_q >= j * HEAD_DIM) & (lane_q < (j + 1) * HEAD_DIM)
        if kw > LANES:
            m = m | ((lane_q >= LANES) & (lane_q % AUG_STRIDE == j))
        q_masks.append(m)

    s_scr = (s_scr0, s_scr1, s_scr2)
    col_max = [None] * n_units

    def scores(i, c, qcat):
        s = lax.dot_general(kk_ref[c * TQ:(c + 1) * TQ, :], qcat, (((1,), (1,)), ((), ())),
                            preferred_element_type=F32)
        if dense_bias or c == i:
            r0 = s_len - (i + 1 - c) * TQ
            g = g_ref[r0:r0 + TQ, :]
            s = s + jnp.concatenate([g] * HEADS_PER_BLOCK, axis=1)
        s_scr[i % len(s_scr)][c * TQ:(c + 1) * TQ, :] = s
        cm = jnp.max(s, axis=0, keepdims=True)
        col_max[i] = cm if col_max[i] is None else jnp.maximum(col_max[i], cm)

    def probs(i, c):
        return jnp.exp2((s_scr[i % len(s_scr)][c * TQ:(c + 1) * TQ, :] - col_max[i]).astype(BF16))

    def weighted_values(c, p, first):
        o = jnp.dot(vt_scr[:, c * TQ:(c + 1) * TQ], p, preferred_element_type=F32)
        acc_scr[...] = o if first else acc_scr[...] + o

    def score_steps(i):
        qb = qq_ref[i * TQ:(i + 1) * TQ, :]
        qcat = jnp.concatenate([jnp.where(q_masks[j], qb, jnp.zeros((), BF16))
                                for j in range(HEADS_PER_BLOCK)], axis=0)
        return [functools.partial(scores, i, c, qcat) for c in range(i + 1)]

    def finish(i):
        res = []
        for j in range(HEADS_PER_BLOCK):
            num = acc_scr[j * HEAD_DIM:(j + 1) * HEAD_DIM, j * TQ:(j + 1) * TQ]
            den = acc_scr[LANES:LANES + 1, j * TQ:(j + 1) * TQ]
            res.append(num / den)
        o_ref[i * TQ:(i + 1) * TQ, :] = jnp.concatenate(res, axis=0).T.astype(o_ref.dtype)
        col_max[i] = None

    lead = len(s_scr) - 1
    for u in range(min(lead, n_units)):
        for step in score_steps(u):
            step()
    for i in range(n_units):
        n_chunks = i + 1
        nxt = score_steps(i + lead) if i + lead < n_units else []
        p_prev = None
        for k in range(max(n_chunks + 1, len(nxt))):
            if k < len(nxt):
                nxt[k]()
            if p_prev is not None:
                weighted_values(k - 1, p_prev, k == 1)
            p_prev = probs(i, k) if k < n_chunks else None
        finish(i)


def _attn_bounded_kernel(qq_ref, kk_ref, v_ref, g_ref, o_ref, vt_scr, p_scr0, p_scr1, *,
                         dense_bias):
    s_len, kw = qq_ref.shape
    n_units = s_len // TQ
    lane_q = lax.broadcasted_iota(jnp.int32, (TQ, kw), 1)
    vt_scr[0:LANES, :] = v_ref[...].T
    vt_scr[LANES:, :] = jnp.ones((ONES_ROWS, s_len), BF16)

    q_masks = []
    for j in range(HEADS_PER_BLOCK):
        m = (lane_q >= j * HEAD_DIM) & (lane_q < (j + 1) * HEAD_DIM)
        if kw > LANES:
            m = m | ((lane_q >= LANES) & (lane_q % AUG_STRIDE == j))
        q_masks.append(m)
    p_scr = (p_scr0, p_scr1)

    def chunk(i, c, qcat):
        s = lax.dot_general(kk_ref[c * TQ:(c + 1) * TQ, :], qcat, (((1,), (1,)), ((), ())),
                            preferred_element_type=F32)
        if dense_bias or c == i:
            r0 = s_len - (i + 1 - c) * TQ
            g = g_ref[r0:r0 + TQ, :]
            s = s + jnp.concatenate([g] * HEADS_PER_BLOCK, axis=1)
        p_scr[i % len(p_scr)][c * TQ:(c + 1) * TQ, :] = jnp.exp2(s).astype(BF16)

    def unit_steps(i):
        qb = qq_ref[i * TQ:(i + 1) * TQ, :]
        qcat = jnp.concatenate([jnp.where(q_masks[j], qb, jnp.zeros((), BF16))
                                for j in range(HEADS_PER_BLOCK)], axis=0)
        return [functools.partial(chunk, i, c, qcat) for c in range(i + 1)]

    def finish(i):
        q1 = (i + 1) * TQ
        o = jnp.dot(vt_scr[:, 0:q1], p_scr[i % len(p_scr)][0:q1, :],
                    preferred_element_type=F32)
        res = []
        for j in range(HEADS_PER_BLOCK):
            num = o[j * HEAD_DIM:(j + 1) * HEAD_DIM, j * TQ:(j + 1) * TQ]
            den = o[LANES:LANES + 1, j * TQ:(j + 1) * TQ]
            res.append(num / den)
        o_ref[i * TQ:(i + 1) * TQ, :] = jnp.concatenate(res, axis=0).T.astype(o_ref.dtype)

    for i in range(n_units):
        steps = unit_steps(i)
        pv_at = min(PV_LAG, len(steps) - 1)
        for k, step in enumerate(steps):
            step()
            if i > 0 and k == pv_at:
                finish(i - 1)
    finish(n_units - 1)


def _ffn_kernel(x_ref, of_ref, od_ref, gof_ref, god_ref, wof_ref, wod_ref, gffn_ref,
                wg_ref, wu_ref, wd_ref, out_ref):
    def norm(v, g):
        return v * lax.rsqrt(jnp.mean(v * v, axis=-1, keepdims=True) + EPS) * g

    rows = x_ref.shape[0] // FFN_SPLIT
    parts = [slice(r * rows, (r + 1) * rows) for r in range(FFN_SPLIT)]
    attn = [(norm(of_ref[p, :].astype(F32), gof_ref[...]).astype(BF16),
             norm(od_ref[p, :].astype(F32), god_ref[...]).astype(BF16)) for p in parts]
    x1 = [x_ref[p, :] + jnp.dot(nf, wof_ref[...], preferred_element_type=F32)
          + jnp.dot(nd, wod_ref[...], preferred_element_type=F32)
          for p, (nf, nd) in zip(parts, attn)]
    gate_up = []
    for v in x1:
        h = norm(v, gffn_ref[...]).astype(BF16)
        gate_up.append((jnp.dot(h, wg_ref[...], preferred_element_type=F32),
                        jnp.dot(h, wu_ref[...], preferred_element_type=F32)))
    for p, v, (a, u) in zip(parts, x1, gate_up):
        g = (a * (1.0 / (1.0 + jnp.exp(-a))) * u).astype(BF16)
        out_ref[p, :] = v + jnp.dot(g, wd_ref[...], preferred_element_type=F32)


def _rope_tables(s_len):
    half = ROPE_DIM // 2
    inv_freq = jnp.power(jnp.float32(ROPE_THETA),
                         -jnp.arange(half, dtype=jnp.float32) * 2.0 / ROPE_DIM)
    ang = jnp.arange(s_len).astype(jnp.float32)[:, None] * inv_freq[None, :]
    cos, sin = jnp.cos(ang), jnp.sin(ang)
    ones = jnp.ones((s_len, HEAD_DIM - ROPE_DIM), F32)
    zeros = jnp.zeros((s_len, HEAD_DIM - ROPE_DIM), F32)
    zh = jnp.zeros((s_len, half), F32)
    c_head = jnp.concatenate([cos, cos, ones], axis=1)
    s1_head = jnp.concatenate([-sin, zh, zeros], axis=1)
    s2_head = jnp.concatenate([zh, sin, zeros], axis=1)
    rep = lambda a: jnp.tile(a, (1, HEADS_PER_BLOCK))
    return rep(c_head), rep(s1_head), rep(s2_head)


def _bias_tables(s_len):
    i = np.arange(TQ)[None, :]
    jp = np.arange(s_len)[:, None]
    delta = i - jp + s_len - TQ
    causal = np.where(delta >= 0, 0.0, NEG).astype(np.float32)
    mult = np.zeros(delta.shape, np.float64)
    for window, dil in DILATION_PAIRS:
        mult += (delta >= 0) & (delta <= window) & (delta % dil == 0)
    dil_bias = np.where(mult > 0, np.log2(np.maximum(mult, 1.0)), NEG).astype(np.float32)
    return causal, dil_bias, float(np.log2(len(DILATION_PAIRS)))


def _resident(shape):
    nd = len(shape)
    return pl.BlockSpec(shape, lambda *_: (0,) * nd, pipeline_mode=pl.Buffered(1))


def _layer(x, g_mix, w_in, b_forget, g_q_fox, g_k_fox, g_q_dil, g_k_dil,
           g_out_fox, g_out_dil, w_out, g_ffn, w_gate, w_up, w_down):
    b_sz, s_len, d = x.shape
    n_heads = b_forget.shape[0]
    w = n_heads * HEAD_DIM
    d_ff = w_gate.shape[1]
    assert w_in.shape[1] == 6 * w + n_heads and w % MXU_DIM == 0
    assert s_len % TM_PROJ == 0 and s_len % TQ == 0 and (b_sz * s_len) % TM_FFN == 0
    assert d % LANES == 0 and n_heads <= AUG_STRIDE

    cols = np.cumsum([0, w, w, w, n_heads, w, w, w])
    w_main = jnp.concatenate([w_in[:, cols[i]:cols[i + 1]] for i in (0, 1, 2, 4, 5, 6)],
                             axis=1).astype(BF16)
    w_fa = jnp.zeros((d, LANES), F32).at[:, :n_heads].set(w_in[:, cols[3]:cols[4]]).astype(BF16)
    b_pad = jnp.zeros((1, LANES), F32).at[0, :n_heads].set(b_forget)
    scale = HEAD_DIM ** -0.5 * LOG2E
    gains = jnp.stack([jnp.tile(g_q_fox * scale, n_heads), jnp.tile(g_k_fox, n_heads),
                       jnp.tile(g_q_dil * scale, n_heads), jnp.tile(g_k_dil, n_heads)])
    head_id = np.arange(MXU_DIM) // HEAD_DIM
    bd = jnp.asarray((head_id[:, None] == head_id[None, :]) / HEAD_DIM, BF16)
    cos_t, s1_t, s2_t = _rope_tables(s_len)
    causal, dil_bias, dil_bias_max = _bias_tables(s_len)

    def logit_bound(gq, gk):
        return (HEAD_DIM * scale * BOUND_SLACK) * jnp.max(jnp.abs(gq)) * jnp.max(jnp.abs(gk))
    bound_fox = logit_bound(g_q_fox, g_k_fox)
    bound_dil = logit_bound(g_q_dil, g_k_dil)
    bounded_ok = 2.0 * jnp.maximum(bound_fox, bound_dil) + dil_bias_max <= EXP2_RANGE
    bnd_pad = jnp.zeros((1, LANES), F32).at[0, :n_heads].set(bound_fox)

    tok = lambda width: pl.BlockSpec((None, TM_PROJ, width), lambda b, t: (b, t, 0))
    pos = pl.BlockSpec((TM_PROJ, LANES), lambda b, t: (t, 0))
    bf = lambda width: jax.ShapeDtypeStruct((b_sz, s_len, width), BF16)
    qqf, kkf, vf, qd, kd, vd = pl.pallas_call(
        functools.partial(_proj_kernel, n_heads=n_heads),
        grid=(b_sz, s_len // TM_PROJ),
        in_specs=[tok(d), _resident((1, d)), _resident((d, 6 * w)), _resident((d, LANES)),
                  _resident((1, LANES)), _resident((1, LANES)), _resident((4, w)),
                  _resident((MXU_DIM, MXU_DIM)), pos, pos, pos],
        out_specs=[tok(2 * w), tok(2 * w), tok(w), tok(w), tok(w), tok(w)],
        out_shape=[bf(2 * w), bf(2 * w), bf(w), bf(w), bf(w), bf(w)],
        scratch_shapes=[pltpu.VMEM((8, LANES), F32)],
        compiler_params=pltpu.CompilerParams(
            dimension_semantics=("arbitrary", "arbitrary"), vmem_limit_bytes=VMEM_LIMIT),
        name="proj",
    )(x, g_mix.reshape(1, d), w_main, w_fa, b_pad, bnd_pad, gains, bd, cos_t, s1_t, s2_t)

    def attention(body, scratch, qq, kk, v, bias, dense_bias, name):
        kw = qq.shape[-1] // (w // LANES)
        blk = lambda width: pl.BlockSpec((None, s_len, width), lambda b, p: (b, 0, p))
        return pl.pallas_call(
            functools.partial(body, dense_bias=dense_bias),
            grid=(b_sz, w // LANES),
            in_specs=[blk(kw), blk(kw), blk(LANES), _resident((s_len, TQ))],
            out_specs=blk(LANES),
            out_shape=bf(w),
            scratch_shapes=scratch,
            compiler_params=pltpu.CompilerParams(
                dimension_semantics=("arbitrary", "arbitrary"), vmem_limit_bytes=VMEM_LIMIT),
            name=name,
        )(qq, kk, v, bias)

    nq = HEADS_PER_BLOCK * TQ
    vt_shape = pltpu.VMEM((LANES + ONES_ROWS, s_len), BF16)
    exact_scratch = [vt_shape] + [pltpu.VMEM((s_len, nq), F32)] * 3 + [
        pltpu.VMEM((LANES + ONES_ROWS, nq), F32)]
    bounded_scratch = [vt_shape] + [pltpu.VMEM((s_len, nq), BF16)] * 2

    def attend_bounded():
        return (attention(_attn_bounded_kernel, bounded_scratch, qqf, kkf, vf,
                          jnp.asarray(causal), False, "attn_fox"),
                attention(_attn_bounded_kernel, bounded_scratch, qd, kd, vd,
                          jnp.asarray(dil_bias) - (bound_dil + dil_bias_max), True, "attn_dil"))

    def attend_exact():
        return (attention(_attn_exact_kernel, exact_scratch, qqf, kkf, vf,
                          jnp.asarray(causal), False, "attn_fox_exact"),
                attention(_attn_exact_kernel, exact_scratch, qd, kd, vd,
                          jnp.asarray(dil_bias), True, "attn_dil_exact"))

    o_fox, o_dil = lax.cond(bounded_ok, attend_bounded, attend_exact)

    n_tok = b_sz * s_len
    rows = lambda width: pl.BlockSpec((TM_FFN, width), lambda t: (t, 0))
    out = pl.pallas_call(
        _ffn_kernel,
        grid=(n_tok // TM_FFN,),
        in_specs=[rows(d), rows(w), rows(w), _resident((1, w)), _resident((1, w)),
                  _resident((w, d)), _resident((w, d)), _resident((1, d)),
                  _resident((d, d_ff)), _resident((d, d_ff)), _resident((d_ff, d))],
        out_specs=rows(d),
        out_shape=jax.ShapeDtypeStruct((n_tok, d), x.dtype),
        compiler_params=pltpu.CompilerParams(
            dimension_semantics=("arbitrary",), vmem_limit_bytes=VMEM_LIMIT),
        name="ffn",
    )(x.reshape(n_tok, d), o_fox.reshape(n_tok, w), o_dil.reshape(n_tok, w),
      g_out_fox.reshape(1, w), g_out_dil.reshape(1, w),
      w_out[:w].astype(BF16), w_out[w:].astype(BF16), g_ffn.reshape(1, d),
      w_gate.astype(BF16), w_up.astype(BF16), w_down.astype(BF16))
    return out.reshape(b_sz, s_len, d)


def kernel(x, g_mix, w_in, b_forget, g_q_fox, g_k_fox, g_q_dil, g_k_dil, g_out_fox, g_out_dil,
           w_out, g_ffn, w_gate, w_up, w_down):
    for l in range(g_mix.shape[0]):
        x = _layer(x, g_mix[l], w_in[l], b_forget[l], g_q_fox[l], g_k_fox[l], g_q_dil[l],
                   g_k_dil[l], g_out_fox[l], g_out_dil[l], w_out[l], g_ffn[l], w_gate[l],
                   w_up[l], w_down[l])
    return x
```

```python
import functools
import math

import jax
import jax.numpy as jnp
import numpy as np
from jax import lax
from jax.experimental import pallas as pl
from jax.experimental.pallas import tpu as pltpu

F32 = jnp.float32
BF16 = jnp.bfloat16

HEAD_DIM = 64
ROPE_DIM = HEAD_DIM // 4
ROPE_THETA = 500000.0
DILATION_PAIRS = ((128, 1), (512, 4), (2048, 16))
EPS = 1e-6
NEG = -1e30
LOG2E = math.log2(math.e)

LANES = 128
MXU_DIM = 256
HEADS_PER_BLOCK = LANES // HEAD_DIM
ONES_ROWS = 16
AUG_TERMS = 3
AUG_STRIDE = 8
Q_ONES = 4 * AUG_STRIDE

TM_PROJ = 1024
PROJ_SPLIT = 2
TQ = 256
TM_FFN = 1024
FFN_SPLIT = 4
VMEM_LIMIT = 56 * 1024 * 1024
PV_LAG = 1
PAIRS_PER_STEP = 2
EXP2_RANGE = 64.0
BOUND_SLACK = 1.02


def _split3(v):
    hi = v.astype(BF16).astype(F32)
    r1 = v - hi
    mid = r1.astype(BF16).astype(F32)
    lo = (r1 - mid).astype(BF16).astype(F32)
    return hi, mid, lo


def _pack3(v):
    hi, mid, lo = _split3(v)
    return hi + pltpu.roll(mid, AUG_STRIDE, 1) + pltpu.roll(lo, 2 * AUG_STRIDE, 1)


def _round_robin(stage_generators):
    active, done = list(stage_generators), object()
    while active:
        active = [g for g in active if next(g, done) is not done]


def _proj_kernel(x_ref, gmix_ref, wmain_ref, wfa_ref, bf_ref, bnd_ref, gains_ref, bd_ref,
                 cos_ref, s1_ref, s2_ref,
                 qqf_ref, kkf_ref, vf_ref, qd_ref, kd_ref, vd_ref, carry_ref, *, n_heads):
    t = pl.program_id(1)
    tm, d = x_ref.shape
    rows = tm // PROJ_SPLIT
    w = n_heads * HEAD_DIM
    n_blk = w // LANES
    lane = lax.broadcasted_iota(jnp.int32, (rows, LANES), 1)
    row = lax.broadcasted_iota(jnp.int32, (rows, LANES), 0)
    term_lane = (lane % AUG_STRIDE < HEADS_PER_BLOCK) & (lane < AUG_TERMS * AUG_STRIDE)
    ones_lane = ((lane % AUG_STRIDE < HEADS_PER_BLOCK) & (lane >= Q_ONES)
                 & (lane < Q_ONES + AUG_TERMS * AUG_STRIDE))

    @pl.when(t == 0)
    def _():
        carry_ref[...] = jnp.zeros_like(carry_ref)

    def mean_squares(p):
        sq = (p * p).astype(BF16)
        parts = [jnp.dot(sq[:, j * MXU_DIM:(j + 1) * MXU_DIM], bd_ref[...],
                         preferred_element_type=F32) for j in range(w // MXU_DIM)]
        return jnp.concatenate(parts, axis=1)

    def rope(xc, rs):
        return (xc * cos_ref[rs, :] + pltpu.roll(xc, LANES - ROPE_DIM // 2, 1) * s1_ref[rs, :]
                + pltpu.roll(xc, ROPE_DIM // 2, 1) * s2_ref[rs, :])

    def row_group(rs):
        x = x_ref[rs, :]
        ms = jnp.mean(x * x, axis=-1, keepdims=True)
        h = (x * lax.rsqrt(ms + EPS) * gmix_ref[...]).astype(BF16)

        def group(i):
            return jnp.dot(h, wmain_ref[:, i * w:(i + 1) * w], preferred_element_type=F32)

        half = d // 2
        fa = (jnp.dot(h[:, :half], wfa_ref[:half, :], preferred_element_type=F32)
              + jnp.dot(h[:, half:], wfa_ref[half:, :], preferred_element_type=F32)
              + bf_ref[...])
        g_q = group(0)
        yield

        logf = jnp.minimum(fa, 0.0) - jnp.log1p(jnp.exp(-jnp.abs(fa)))
        logf = jnp.where(lane < n_heads, logf * LOG2E, 0.0)
        g_k = group(1)
        yield

        ctile = logf
        shift = 1
        while shift < rows:
            ctile = ctile + jnp.where(row >= shift, pltpu.roll(ctile, shift, 0), 0.0)
            shift *= 2
        c = jnp.where(lane < n_heads, ctile + carry_ref[0:1, :], 0.0)
        carry_ref[...] = jnp.broadcast_to(c[rows - 1:rows, :], carry_ref.shape)
        cq = _pack3(c)
        ck = _pack3(jnp.where(lane < n_heads, c + bnd_ref[...], 0.0))

        def gate_lanes(p):
            shift = HEADS_PER_BLOCK * p
            qa = jnp.where(term_lane, pltpu.roll(cq, (LANES - shift) % LANES, 1),
                           jnp.where(ones_lane, 1.0, 0.0))
            ka = jnp.where(ones_lane, -pltpu.roll(ck, (Q_ONES - shift) % LANES, 1),
                           jnp.where(term_lane, 1.0, 0.0))
            return qa.astype(BF16), ka.astype(BF16)

        ms_q = mean_squares(g_q)
        g_v = group(2)
        yield
        qf = g_q * lax.rsqrt(ms_q + EPS) * gains_ref[0:1, :]
        ms_k = mean_squares(g_k)
        vf_ref[rs, :] = g_v.astype(BF16)
        g_qd = group(3)
        yield
        kf = g_k * lax.rsqrt(ms_k + EPS) * gains_ref[1:2, :]
        for j in range(n_blk):
            qa, ka = gate_lanes(j)
            qqf_ref[rs, 2 * j * LANES:(2 * j + 1) * LANES] = qf[:, j * LANES:(j + 1) * LANES].astype(BF16)
            qqf_ref[rs, (2 * j + 1) * LANES:(2 * j + 2) * LANES] = qa
            kkf_ref[rs, 2 * j * LANES:(2 * j + 1) * LANES] = kf[:, j * LANES:(j + 1) * LANES].astype(BF16)
            kkf_ref[rs, (2 * j + 1) * LANES:(2 * j + 2) * LANES] = ka

        ms_qd = mean_squares(g_qd)
        g_kd = group(4)
        yield
        qd = g_qd * lax.rsqrt(ms_qd + EPS) * gains_ref[2:3, :]
        ms_kd = mean_squares(g_kd)
        g_vd = group(5)
        yield
        kd = g_kd * lax.rsqrt(ms_kd + EPS) * gains_ref[3:4, :]
        for j in range(n_blk):
            qd_ref[rs, j * LANES:(j + 1) * LANES] = rope(qd[:, j * LANES:(j + 1) * LANES], rs).astype(BF16)
            kd_ref[rs, j * LANES:(j + 1) * LANES] = rope(kd[:, j * LANES:(j + 1) * LANES], rs).astype(BF16)
        vd_ref[rs, :] = g_vd.astype(BF16)

    _round_robin(row_group(slice(r * rows, (r + 1) * rows)) for r in range(PROJ_SPLIT))


def _attn_exact_kernel(qq_ref, kk_ref, v_ref, g_ref, o_ref, vt_scr, s_scr0, s_scr1, s_scr2,
                       acc_scr, *, dense_bias):
    s_len, kw = qq_ref.shape
    n_units = s_len // TQ
    lane_q = lax.broadcasted_iota(jnp.int32, (TQ, kw), 1)
    vt_scr[0:LANES, :] = v_ref[...].T
    vt_scr[LANES:, :] = jnp.ones((ONES_ROWS, s_len), BF16)

    q_masks = []
    for j in range(HEADS_PER_BLOCK):
        m = (lane_q >= j * HEAD_DIM) & (lane_q < (j + 1) * HEAD_DIM)
        if kw > LANES:
            m = m | ((lane_q >= LANES) & (lane_q % AUG_STRIDE == j))
        q_masks.append(m)

    s_scr = (s_scr0, s_scr1, s_scr2)
    col_max = [None] * n_units

    def scores(i, c, qcat):
        s = lax.dot_general(kk_ref[c * TQ:(c + 1) * TQ, :], qcat, (((1,), (1,)), ((), ())),
                            preferred_element_type=F32)
        if dense_bias or c == i:
            r0 = s_len - (i + 1 - c) * TQ
            g = g_ref[r0:r0 + TQ, :]
            s = s + jnp.concatenate([g] * HEADS_PER_BLOCK, axis=1)
        s_scr[i % len(s_scr)][c * TQ:(c + 1) * TQ, :] = s
        cm = jnp.max(s, axis=0, keepdims=True)
        col_max[i] = cm if col_max[i] is None else jnp.maximum(col_max[i], cm)

    def probs(i, c):
        return jnp.exp2((s_scr[i % len(s_scr)][c * TQ:(c + 1) * TQ, :] - col_max[i]).astype(BF16))

    def weighted_values(c, p, first):
        o = jnp.dot(vt_scr[:, c * TQ:(c + 1) * TQ], p, preferred_element_type=F32)
        acc_scr[...] = o if first else acc_scr[...] + o

    def score_steps(i):
        qb = qq_ref[i * TQ:(i + 1) * TQ, :]
        qcat = jnp.concatenate([jnp.where(q_masks[j], qb, jnp.zeros((), BF16))
                                for j in range(HEADS_PER_BLOCK)], axis=0)
        return [functools.partial(scores, i, c, qcat) for c in range(i + 1)]

    def finish(i):
        res = []
        for j in range(HEADS_PER_BLOCK):
            num = acc_scr[j * HEAD_DIM:(j + 1) * HEAD_DIM, j * TQ:(j + 1) * TQ]
            den = acc_scr[LANES:LANES + 1, j * TQ:(j + 1) * TQ]
            res.append(num / den)
        o_ref[i * TQ:(i + 1) * TQ, :] = jnp.concatenate(res, axis=0).T.astype(o_ref.dtype)
        col_max[i] = None

    lead = len(s_scr) - 1
    for u in range(min(lead, n_units)):
        for step in score_steps(u):
            step()
    for i in range(n_units):
        n_chunks = i + 1
        nxt = score_steps(i + lead) if i + lead < n_units else []
        p_prev = None
        for k in range(max(n_chunks + 1, len(nxt))):
            if k < len(nxt):
                nxt[k]()
            if p_prev is not None:
                weighted_values(k - 1, p_prev, k == 1)
            p_prev = probs(i, k) if k < n_chunks else None
        finish(i)


def _attn_bounded_kernel(qq_ref, kk_ref, v_ref, g_ref, o_ref, vt_scr, p_scr0, p_scr1, *,
                         dense_bias):
    s_len = qq_ref.shape[0]
    n_pairs = v_ref.shape[1] // LANES
    kw = qq_ref.shape[1] // n_pairs
    n_blocks = s_len // TQ
    lane_q = lax.broadcasted_iota(jnp.int32, (TQ, kw), 1)
    for pr in range(n_pairs):
        vt_scr[pr, 0:LANES, :] = v_ref[:, pr * LANES:(pr + 1) * LANES].T
        vt_scr[pr, LANES:, :] = jnp.ones((ONES_ROWS, s_len), BF16)

    q_masks = []
    for j in range(HEADS_PER_BLOCK):
        m = (lane_q >= j * HEAD_DIM) & (lane_q < (j + 1) * HEAD_DIM)
        if kw > LANES:
            m = m | ((lane_q >= LANES) & (lane_q % AUG_STRIDE == j))
        q_masks.append(m)
    p_scr = (p_scr0, p_scr1)
    units = [(pr, i) for pr in range(n_pairs) for i in range(n_blocks)]

    def chunk(n, c, qcat):
        pr, i = units[n]
        s = lax.dot_general(kk_ref[c * TQ:(c + 1) * TQ, pr * kw:(pr + 1) * kw], qcat,
                            (((1,), (1,)), ((), ())),
                            preferred_element_type=F32)
        if dense_bias or c == i:
            r0 = s_len - (i + 1 - c) * TQ
            g = g_ref[r0:r0 + TQ, :]
            s = s + jnp.concatenate([g] * HEADS_PER_BLOCK, axis=1)
        p_scr[n % len(p_scr)][c * TQ:(c + 1) * TQ, :] = jnp.exp2(s).astype(BF16)

    def unit_steps(n):
        pr, i = units[n]
        qb = qq_ref[i * TQ:(i + 1) * TQ, pr * kw:(pr + 1) * kw]
        qcat = jnp.concatenate([jnp.where(q_masks[j], qb, jnp.zeros((), BF16))
                                for j in range(HEADS_PER_BLOCK)], axis=0)
        return [functools.partial(chunk, n, c, qcat) for c in range(i + 1)]

    def finish(n):
        pr, i = units[n]
        q1 = (i + 1) * TQ
        o = jnp.dot(vt_scr[pr, :, 0:q1], p_scr[n % len(p_scr)][0:q1, :],
                    preferred_element_type=F32)
        res = []
        for j in range(HEADS_PER_BLOCK):
            num = o[j * HEAD_DIM:(j + 1) * HEAD_DIM, j * TQ:(j + 1) * TQ]
            den = o[LANES:LANES + 1, j * TQ:(j + 1) * TQ]
            res.append(num / den)
        o_ref[i * TQ:(i + 1) * TQ, pr * LANES:(pr + 1) * LANES] = (
            jnp.concatenate(res, axis=0).T.astype(o_ref.dtype))

    for n in range(len(units)):
        steps = unit_steps(n)
        pv_at = min(PV_LAG, len(steps) - 1)
        for k, step in enumerate(steps):
            step()
            if n > 0 and k == pv_at:
                finish(n - 1)
    finish(len(units) - 1)


def _staggered(stage_generators):
    waiting, active, done = list(stage_generators), [], object()
    while waiting or active:
        if waiting:
            active.append(waiting.pop(0))
        active = [g for g in active if next(g, done) is not done]


def _ffn_kernel(x_ref, of_ref, od_ref, gof_ref, god_ref, wof_ref, wod_ref, gffn_ref,
                wg_ref, wu_ref, wd_ref, out_ref):
    def norm(v, g):
        return v * lax.rsqrt(jnp.mean(v * v, axis=-1, keepdims=True) + EPS) * g

    rows = x_ref.shape[0] // FFN_SPLIT

    def row_group(rs):
        nf = norm(of_ref[rs, :].astype(F32), gof_ref[...]).astype(BF16)
        nd = norm(od_ref[rs, :].astype(F32), god_ref[...]).astype(BF16)
        x1 = (x_ref[rs, :] + jnp.dot(nf, wof_ref[...], preferred_element_type=F32)
              + jnp.dot(nd, wod_ref[...], preferred_element_type=F32))
        yield
        h = norm(x1, gffn_ref[...]).astype(BF16)
        a = jnp.dot(h, wg_ref[...], preferred_element_type=F32)
        u = jnp.dot(h, wu_ref[...], preferred_element_type=F32)
        yield
        g = (a * (1.0 / (1.0 + jnp.exp(-a))) * u).astype(BF16)
        out_ref[rs, :] = x1 + jnp.dot(g, wd_ref[...], preferred_element_type=F32)

    _staggered(row_group(slice(r * rows, (r + 1) * rows)) for r in range(FFN_SPLIT))


def _rope_tables(s_len):
    half = ROPE_DIM // 2
    inv_freq = jnp.power(jnp.float32(ROPE_THETA),
                         -jnp.arange(half, dtype=jnp.float32) * 2.0 / ROPE_DIM)
    ang = jnp.arange(s_len).astype(jnp.float32)[:, None] * inv_freq[None, :]
    cos, sin = jnp.cos(ang), jnp.sin(ang)
    ones = jnp.ones((s_len, HEAD_DIM - ROPE_DIM), F32)
    zeros = jnp.zeros((s_len, HEAD_DIM - ROPE_DIM), F32)
    zh = jnp.zeros((s_len, half), F32)
    c_head = jnp.concatenate([cos, cos, ones], axis=1)
    s1_head = jnp.concatenate([-sin, zh, zeros], axis=1)
    s2_head = jnp.concatenate([zh, sin, zeros], axis=1)
    rep = lambda a: jnp.tile(a, (1, HEADS_PER_BLOCK))
    return rep(c_head), rep(s1_head), rep(s2_head)


def _bias_tables(s_len):
    i = np.arange(TQ)[None, :]
    jp = np.arange(s_len)[:, None]
    delta = i - jp + s_len - TQ
    causal = np.where(delta >= 0, 0.0, NEG).astype(np.float32)
    mult = np.zeros(delta.shape, np.float64)
    for window, dil in DILATION_PAIRS:
        mult += (delta >= 0) & (delta <= window) & (delta % dil == 0)
    dil_bias = np.where(mult > 0, np.log2(np.maximum(mult, 1.0)), NEG).astype(np.float32)
    return causal, dil_bias, float(np.log2(len(DILATION_PAIRS)))


def _resident(shape):
    nd = len(shape)
    return pl.BlockSpec(shape, lambda *_: (0,) * nd, pipeline_mode=pl.Buffered(1))


def _layer(x, g_mix, w_in, b_forget, g_q_fox, g_k_fox, g_q_dil, g_k_dil,
           g_out_fox, g_out_dil, w_out, g_ffn, w_gate, w_up, w_down):
    b_sz, s_len, d = x.shape
    n_heads = b_forget.shape[0]
    w = n_heads * HEAD_DIM
    d_ff = w_gate.shape[1]
    assert w_in.shape[1] == 6 * w + n_heads and w % MXU_DIM == 0
    assert s_len % TM_PROJ == 0 and s_len % TQ == 0 and (b_sz * s_len) % TM_FFN == 0
    assert d % LANES == 0 and n_heads <= AUG_STRIDE

    cols = np.cumsum([0, w, w, w, n_heads, w, w, w])
    w_main = jnp.concatenate([w_in[:, cols[i]:cols[i + 1]] for i in (0, 1, 2, 4, 5, 6)],
                             axis=1).astype(BF16)
    w_fa = jnp.zeros((d, LANES), F32).at[:, :n_heads].set(w_in[:, cols[3]:cols[4]]).astype(BF16)
    b_pad = jnp.zeros((1, LANES), F32).at[0, :n_heads].set(b_forget)
    scale = HEAD_DIM ** -0.5 * LOG2E
    gains = jnp.stack([jnp.tile(g_q_fox * scale, n_heads), jnp.tile(g_k_fox, n_heads),
                       jnp.tile(g_q_dil * scale, n_heads), jnp.tile(g_k_dil, n_heads)])
    head_id = np.arange(MXU_DIM) // HEAD_DIM
    bd = jnp.asarray((head_id[:, None] == head_id[None, :]) / HEAD_DIM, BF16)
    cos_t, s1_t, s2_t = _rope_tables(s_len)
    causal, dil_bias, dil_bias_max = _bias_tables(s_len)

    def logit_bound(gq, gk):
        return (HEAD_DIM * scale * BOUND_SLACK) * jnp.max(jnp.abs(gq)) * jnp.max(jnp.abs(gk))
    bound_fox = logit_bound(g_q_fox, g_k_fox)
    bound_dil = logit_bound(g_q_dil, g_k_dil)
    bounded_ok = 2.0 * jnp.maximum(bound_fox, bound_dil) + dil_bias_max <= EXP2_RANGE
    bnd_pad = jnp.zeros((1, LANES), F32).at[0, :n_heads].set(bound_fox)

    tok = lambda width: pl.BlockSpec((None, TM_PROJ, width), lambda b, t: (b, t, 0))
    pos = pl.BlockSpec((TM_PROJ, LANES), lambda b, t: (t, 0))
    bf = lambda width: jax.ShapeDtypeStruct((b_sz, s_len, width), BF16)
    qqf, kkf, vf, qd, kd, vd = pl.pallas_call(
        functools.partial(_proj_kernel, n_heads=n_heads),
        grid=(b_sz, s_len // TM_PROJ),
        in_specs=[tok(d), _resident((1, d)), _resident((d, 6 * w)), _resident((d, LANES)),
                  _resident((1, LANES)), _resident((1, LANES)), _resident((4, w)),
                  _resident((MXU_DIM, MXU_DIM)), pos, pos, pos],
        out_specs=[tok(2 * w), tok(2 * w), tok(w), tok(w), tok(w), tok(w)],
        out_shape=[bf(2 * w), bf(2 * w), bf(w), bf(w), bf(w), bf(w)],
        scratch_shapes=[pltpu.VMEM((8, LANES), F32)],
        compiler_params=pltpu.CompilerParams(
            dimension_semantics=("arbitrary", "arbitrary"), vmem_limit_bytes=VMEM_LIMIT),
        name="proj",
    )(x, g_mix.reshape(1, d), w_main, w_fa, b_pad, bnd_pad, gains, bd, cos_t, s1_t, s2_t)

    def attention(body, scratch, pairs, qq, kk, v, bias, dense_bias, name):
        kw = qq.shape[-1] // (w // LANES) * pairs
        blk = lambda width: pl.BlockSpec((None, s_len, width), lambda b, p: (b, 0, p))
        return pl.pallas_call(
            functools.partial(body, dense_bias=dense_bias),
            grid=(b_sz, w // LANES // pairs),
            in_specs=[blk(kw), blk(kw), blk(pairs * LANES), _resident((s_len, TQ))],
            out_specs=blk(pairs * LANES),
            out_shape=bf(w),
            scratch_shapes=scratch,
            compiler_params=pltpu.CompilerParams(
                dimension_semantics=("arbitrary", "arbitrary"), vmem_limit_bytes=VMEM_LIMIT),
            name=name,
        )(qq, kk, v, bias)

    nq = HEADS_PER_BLOCK * TQ
    vt_rows = LANES + ONES_ROWS
    exact_scratch = [pltpu.VMEM((vt_rows, s_len), BF16)] + [pltpu.VMEM((s_len, nq), F32)] * 3 + [
        pltpu.VMEM((vt_rows, nq), F32)]
    bounded_scratch = [pltpu.VMEM((PAIRS_PER_STEP, vt_rows, s_len), BF16)] + [
        pltpu.VMEM((s_len, nq), BF16)] * 2

    def attend_bounded():
        return (attention(_attn_bounded_kernel, bounded_scratch, PAIRS_PER_STEP, qqf, kkf, vf,
                          jnp.asarray(causal), False, "attn_fox"),
                attention(_attn_bounded_kernel, bounded_scratch, PAIRS_PER_STEP, qd, kd, vd,
                          jnp.asarray(dil_bias) - (bound_dil + dil_bias_max), True, "attn_dil"))

    def attend_exact():
        return (attention(_attn_exact_kernel, exact_scratch, 1, qqf, kkf, vf,
                          jnp.asarray(causal), False, "attn_fox_exact"),
                attention(_attn_exact_kernel, exact_scratch, 1, qd, kd, vd,
                          jnp.asarray(dil_bias), True, "attn_dil_exact"))

    o_fox, o_dil = lax.cond(bounded_ok, attend_bounded, attend_exact)

    n_tok = b_sz * s_len
    rows = lambda width: pl.BlockSpec((TM_FFN, width), lambda t: (t, 0))
    out = pl.pallas_call(
        _ffn_kernel,
        grid=(n_tok // TM_FFN,),
        in_specs=[rows(d), rows(w), rows(w), _resident((1, w)), _resident((1, w)),
                  _resident((w, d)), _resident((w, d)), _resident((1, d)),
                  _resident((d, d_ff)), _resident((d, d_ff)), _resident((d_ff, d))],
        out_specs=rows(d),
        out_shape=jax.ShapeDtypeStruct((n_tok, d), x.dtype),
        compiler_params=pltpu.CompilerParams(
            dimension_semantics=("arbitrary",), vmem_limit_bytes=VMEM_LIMIT),
        name="ffn",
    )(x.reshape(n_tok, d), o_fox.reshape(n_tok, w), o_dil.reshape(n_tok, w),
      g_out_fox.reshape(1, w), g_out_dil.reshape(1, w),
      w_out[:w].astype(BF16), w_out[w:].astype(BF16), g_ffn.reshape(1, d),
      w_gate.astype(BF16), w_up.astype(BF16), w_down.astype(BF16))
    return out.reshape(b_sz, s_len, d)


def kernel(x, g_mix, w_in, b_forget, g_q_fox, g_k_fox, g_q_dil, g_k_dil, g_out_fox, g_out_dil,
           w_out, g_ffn, w_gate, w_up, w_down):
    for l in range(g_mix.shape[0]):
        x = _layer(x, g_mix[l], w_in[l], b_forget[l], g_q_fox[l], g_k_fox[l], g_q_dil[l],
                   g_k_dil[l], g_out_fox[l], g_out_dil[l], w_out[l], g_ffn[l], w_gate[l],
                   w_up[l], w_down[l])
    return x
```

```python
import functools
import math

import jax
import jax.numpy as jnp
import numpy as np
from jax import lax
from jax.experimental import pallas as pl
from jax.experimental.pallas import tpu as pltpu

F32 = jnp.float32
BF16 = jnp.bfloat16

HEAD_DIM = 64
ROPE_DIM = HEAD_DIM // 4
ROPE_THETA = 500000.0
DILATION_PAIRS = ((128, 1), (512, 4), (2048, 16))
EPS = 1e-6
NEG = -1e30
LOG2E = math.log2(math.e)

LANES = 128
SUBLANES = 8
MXU_DIM = 256
HEADS_PER_BLOCK = LANES // HEAD_DIM
ONES_ROWS = 16
AUG_TERMS = 3
AUG_STRIDE = 8
Q_ONES = 4 * AUG_STRIDE

TM_PROJ = 1024
PROJ_SPLIT = 2
TQ = 256
TM_FFN = 1024
FFN_SPLIT = 4
VMEM_LIMIT = 56 * 1024 * 1024
PV_LAG = 1
PAIRS_PER_STEP = 2
EXP2_RANGE = 64.0
BOUND_SLACK = 1.02


def _split3(v):
    hi = v.astype(BF16).astype(F32)
    r1 = v - hi
    mid = r1.astype(BF16).astype(F32)
    lo = (r1 - mid).astype(BF16).astype(F32)
    return hi, mid, lo


def _pack3(v):
    hi, mid, lo = _split3(v)
    return hi + pltpu.roll(mid, AUG_STRIDE, 1) + pltpu.roll(lo, 2 * AUG_STRIDE, 1)


def _round_robin(stage_generators):
    active, done = list(stage_generators), object()
    while active:
        active = [g for g in active if next(g, done) is not done]


def _proj_kernel(x_ref, gmix_ref, wmain_ref, wfa_ref, bf_ref, bnd_ref, gains_ref, bd_ref,
                 cos_ref, s1_ref, s2_ref,
                 qqf_ref, kkf_ref, vf_ref, qd_ref, kd_ref, vd_ref, carry_ref, *, n_heads):
    t = pl.program_id(1)
    tm, d = x_ref.shape
    rows = tm // PROJ_SPLIT
    w = n_heads * HEAD_DIM
    n_blk = w // LANES
    lane = lax.broadcasted_iota(jnp.int32, (rows, LANES), 1)
    row = lax.broadcasted_iota(jnp.int32, (rows, LANES), 0)
    term_lane = (lane % AUG_STRIDE < HEADS_PER_BLOCK) & (lane < AUG_TERMS * AUG_STRIDE)
    ones_lane = ((lane % AUG_STRIDE < HEADS_PER_BLOCK) & (lane >= Q_ONES)
                 & (lane < Q_ONES + AUG_TERMS * AUG_STRIDE))

    @pl.when(t == 0)
    def _():
        carry_ref[...] = jnp.zeros_like(carry_ref)

    def mean_squares(p):
        sq = (p * p).astype(BF16)
        parts = [jnp.dot(sq[:, j * MXU_DIM:(j + 1) * MXU_DIM], bd_ref[...],
                         preferred_element_type=F32) for j in range(w // MXU_DIM)]
        return jnp.concatenate(parts, axis=1)

    def rope(xc, rs):
        return (xc * cos_ref[rs, :] + pltpu.roll(xc, LANES - ROPE_DIM // 2, 1) * s1_ref[rs, :]
                + pltpu.roll(xc, ROPE_DIM // 2, 1) * s2_ref[rs, :])

    def row_group(rs):
        x = x_ref[rs, :]
        ms = jnp.mean(x * x, axis=-1, keepdims=True)
        h = (x * lax.rsqrt(ms + EPS) * gmix_ref[...]).astype(BF16)

        def group(i):
            return jnp.dot(h, wmain_ref[:, i * w:(i + 1) * w], preferred_element_type=F32)

        half = d // 2
        fa = (jnp.dot(h[:, :half], wfa_ref[:half, :], preferred_element_type=F32)
              + jnp.dot(h[:, half:], wfa_ref[half:, :], preferred_element_type=F32)
              + bf_ref[...])
        g_q = group(0)
        yield

        logf = jnp.minimum(fa, 0.0) - jnp.log1p(jnp.exp(-jnp.abs(fa)))
        logf = jnp.where(lane < n_heads, logf * LOG2E, 0.0)
        g_k = group(1)
        yield

        ctile = logf
        shift = 1
        while shift < rows:
            ctile = ctile + jnp.where(row >= shift, pltpu.roll(ctile, shift, 0), 0.0)
            shift *= 2
        c = jnp.where(lane < n_heads, ctile + carry_ref[0:1, :], 0.0)
        carry_ref[...] = jnp.broadcast_to(c[rows - 1:rows, :], carry_ref.shape)
        cq = _pack3(c)
        ck = _pack3(jnp.where(lane < n_heads, c + bnd_ref[...], 0.0))

        def gate_lanes(p):
            shift = HEADS_PER_BLOCK * p
            qa = jnp.where(term_lane, pltpu.roll(cq, (LANES - shift) % LANES, 1),
                           jnp.where(ones_lane, 1.0, 0.0))
            ka = jnp.where(ones_lane, -pltpu.roll(ck, (Q_ONES - shift) % LANES, 1),
                           jnp.where(term_lane, 1.0, 0.0))
            return qa.astype(BF16), ka.astype(BF16)

        ms_q = mean_squares(g_q)
        g_v = group(2)
        yield
        qf = g_q * lax.rsqrt(ms_q + EPS) * gains_ref[0:1, :]
        ms_k = mean_squares(g_k)
        vf_ref[rs, :] = g_v.astype(BF16)
        g_qd = group(3)
        yield
        kf = g_k * lax.rsqrt(ms_k + EPS) * gains_ref[1:2, :]
        for j in range(n_blk):
            qa, ka = gate_lanes(j)
            qqf_ref[rs, 2 * j * LANES:(2 * j + 1) * LANES] = qf[:, j * LANES:(j + 1) * LANES].astype(BF16)
            qqf_ref[rs, (2 * j + 1) * LANES:(2 * j + 2) * LANES] = qa
            kkf_ref[rs, 2 * j * LANES:(2 * j + 1) * LANES] = kf[:, j * LANES:(j + 1) * LANES].astype(BF16)
            kkf_ref[rs, (2 * j + 1) * LANES:(2 * j + 2) * LANES] = ka

        ms_qd = mean_squares(g_qd)
        g_kd = group(4)
        yield
        qd = g_qd * lax.rsqrt(ms_qd + EPS) * gains_ref[2:3, :]
        ms_kd = mean_squares(g_kd)
        g_vd = group(5)
        yield
        kd = g_kd * lax.rsqrt(ms_kd + EPS) * gains_ref[3:4, :]
        for j in range(n_blk):
            qd_ref[rs, j * LANES:(j + 1) * LANES] = rope(qd[:, j * LANES:(j + 1) * LANES], rs).astype(BF16)
            kd_ref[rs, j * LANES:(j + 1) * LANES] = rope(kd[:, j * LANES:(j + 1) * LANES], rs).astype(BF16)
        vd_ref[rs, :] = g_vd.astype(BF16)

    _round_robin(row_group(slice(r * rows, (r + 1) * rows)) for r in range(PROJ_SPLIT))


def _attn_exact_kernel(qq_ref, kk_ref, v_ref, g_ref, o_ref, vt_scr, s_scr0, s_scr1, s_scr2,
                       acc_scr, *, dense_bias):
    s_len, kw = qq_ref.shape
    n_units = s_len // TQ
    lane_q = lax.broadcasted_iota(jnp.int32, (TQ, kw), 1)
    vt_scr[0:LANES, :] = v_ref[...].T
    vt_scr[LANES:, :] = jnp.ones((ONES_ROWS, s_len), BF16)

    q_masks = []
    for j in range(HEADS_PER_BLOCK):
        m = (lane_q >= j * HEAD_DIM) & (lane_q < (j + 1) * HEAD_DIM)
        if kw > LANES:
            m = m | ((lane_q >= LANES) & (lane_q % AUG_STRIDE == j))
        q_masks.append(m)

    s_scr = (s_scr0, s_scr1, s_scr2)
    col_max = [None] * n_units

    def scores(i, c, qcat):
        s = lax.dot_general(kk_ref[c * TQ:(c + 1) * TQ, :], qcat, (((1,), (1,)), ((), ())),
                            preferred_element_type=F32)
        if dense_bias or c == i:
            r0 = s_len - (i + 1 - c) * TQ
            g = g_ref[r0:r0 + TQ, :]
            s = s + jnp.concatenate([g] * HEADS_PER_BLOCK, axis=1)
        s_scr[i % len(s_scr)][c * TQ:(c + 1) * TQ, :] = s
        cm = jnp.max(s, axis=0, keepdims=True)
        col_max[i] = cm if col_max[i] is None else jnp.maximum(col_max[i], cm)

    def probs(i, c):
        return jnp.exp2((s_scr[i % len(s_scr)][c * TQ:(c + 1) * TQ, :] - col_max[i]).astype(BF16))

    def weighted_values(c, p, first):
        o = jnp.dot(vt_scr[:, c * TQ:(c + 1) * TQ], p, preferred_element_type=F32)
        acc_scr[...] = o if first else acc_scr[...] + o

    def score_steps(i):
        qb = qq_ref[i * TQ:(i + 1) * TQ, :]
        qcat = jnp.concatenate([jnp.where(q_masks[j], qb, jnp.zeros((), BF16))
                                for j in range(HEADS_PER_BLOCK)], axis=0)
        return [functools.partial(scores, i, c, qcat) for c in range(i + 1)]

    def finish(i):
        res = []
        for j in range(HEADS_PER_BLOCK):
            num = acc_scr[j * HEAD_DIM:(j + 1) * HEAD_DIM, j * TQ:(j + 1) * TQ]
            den = acc_scr[LANES:LANES + 1, j * TQ:(j + 1) * TQ]
            res.append(num / den)
        o_ref[i * TQ:(i + 1) * TQ, :] = jnp.concatenate(res, axis=0).T.astype(o_ref.dtype)
        col_max[i] = None

    lead = len(s_scr) - 1
    for u in range(min(lead, n_units)):
        for step in score_steps(u):
            step()
    for i in range(n_units):
        n_chunks = i + 1
        nxt = score_steps(i + lead) if i + lead < n_units else []
        p_prev = None
        for k in range(max(n_chunks + 1, len(nxt))):
            if k < len(nxt):
                nxt[k]()
            if p_prev is not None:
                weighted_values(k - 1, p_prev, k == 1)
            p_prev = probs(i, k) if k < n_chunks else None
        finish(i)


def _attn_bounded_kernel(qq_ref, kk_ref, v_ref, g_ref, o_ref, vt_scr, p_scr0, p_scr1, *,
                         dense_bias):
    s_len = qq_ref.shape[0]
    n_pairs = v_ref.shape[1] // LANES
    kw = qq_ref.shape[1] // n_pairs
    n_blocks = s_len // TQ
    lane_q = lax.broadcasted_iota(jnp.int32, (TQ, kw), 1)
    for pr in range(n_pairs):
        vt_scr[pr] = v_ref[:, pr * LANES:(pr + 1) * LANES].T

    q_masks = []
    for j in range(HEADS_PER_BLOCK):
        m = (lane_q >= j * HEAD_DIM) & (lane_q < (j + 1) * HEAD_DIM)
        if kw > LANES:
            m = m | ((lane_q >= LANES) & (lane_q % AUG_STRIDE == j))
        q_masks.append(m)
    p_scr = (p_scr0, p_scr1)
    units = [(pr, i) for pr in range(n_pairs) for i in range(n_blocks)]
    col_sum = [None] * len(units)

    def chunk(n, c, qcat):
        pr, i = units[n]
        s = lax.dot_general(kk_ref[c * TQ:(c + 1) * TQ, pr * kw:(pr + 1) * kw], qcat,
                            (((1,), (1,)), ((), ())),
                            preferred_element_type=F32)
        if dense_bias or c == i:
            r0 = s_len - (i + 1 - c) * TQ
            g = g_ref[r0:r0 + TQ, :]
            s = s + jnp.concatenate([g] * HEADS_PER_BLOCK, axis=1)
        p = jnp.exp2(s)
        part = jnp.sum(p.reshape(TQ // SUBLANES, SUBLANES, p.shape[1]), axis=0)
        col_sum[n] = part if col_sum[n] is None else col_sum[n] + part
        p_scr[n % len(p_scr)][c * TQ:(c + 1) * TQ, :] = p.astype(BF16)

    def unit_steps(n):
        pr, i = units[n]
        qb = qq_ref[i * TQ:(i + 1) * TQ, pr * kw:(pr + 1) * kw]
        qcat = jnp.concatenate([jnp.where(q_masks[j], qb, jnp.zeros((), BF16))
                                for j in range(HEADS_PER_BLOCK)], axis=0)
        return [functools.partial(chunk, n, c, qcat) for c in range(i + 1)]

    def finish(n):
        pr, i = units[n]
        q1 = (i + 1) * TQ
        o = jnp.dot(vt_scr[pr, :, 0:q1], p_scr[n % len(p_scr)][0:q1, :],
                    preferred_element_type=F32)
        sums = jnp.sum(col_sum[n], axis=0, keepdims=True)
        col_sum[n] = None
        res = []
        for j in range(HEADS_PER_BLOCK):
            num = o[j * HEAD_DIM:(j + 1) * HEAD_DIM, j * TQ:(j + 1) * TQ]
            res.append(num / sums[:, j * TQ:(j + 1) * TQ])
        o_ref[i * TQ:(i + 1) * TQ, pr * LANES:(pr + 1) * LANES] = (
            jnp.concatenate(res, axis=0).T.astype(o_ref.dtype))

    for n in range(len(units)):
        steps = unit_steps(n)
        pv_at = min(PV_LAG, len(steps) - 1)
        for k, step in enumerate(steps):
            step()
            if n > 0 and k == pv_at:
                finish(n - 1)
    finish(len(units) - 1)


def _staggered(stage_generators):
    waiting, active, done = list(stage_generators), [], object()
    while waiting or active:
        if waiting:
            active.append(waiting.pop(0))
        active = [g for g in active if next(g, done) is not done]


def _ffn_kernel(x_ref, of_ref, od_ref, gof_ref, god_ref, wof_ref, wod_ref, gffn_ref,
                wg_ref, wu_ref, wd_ref, out_ref):
    def norm(v, g):
        return v * lax.rsqrt(jnp.mean(v * v, axis=-1, keepdims=True) + EPS) * g

    rows = x_ref.shape[0] // FFN_SPLIT

    def row_group(rs):
        nf = norm(of_ref[rs, :].astype(F32), gof_ref[...]).astype(BF16)
        nd = norm(od_ref[rs, :].astype(F32), god_ref[...]).astype(BF16)
        x1 = (x_ref[rs, :] + jnp.dot(nf, wof_ref[...], preferred_element_type=F32)
              + jnp.dot(nd, wod_ref[...], preferred_element_type=F32))
        yield
        h = norm(x1, gffn_ref[...]).astype(BF16)
        a = jnp.dot(h, wg_ref[...], preferred_element_type=F32)
        u = jnp.dot(h, wu_ref[...], preferred_element_type=F32)
        yield
        g = (a * (1.0 / (1.0 + jnp.exp(-a))) * u).astype(BF16)
        out_ref[rs, :] = x1 + jnp.dot(g, wd_ref[...], preferred_element_type=F32)

    _staggered(row_group(slice(r * rows, (r + 1) * rows)) for r in range(FFN_SPLIT))


def _rope_tables(s_len):
    half = ROPE_DIM // 2
    inv_freq = jnp.power(jnp.float32(ROPE_THETA),
                         -jnp.arange(half, dtype=jnp.float32) * 2.0 / ROPE_DIM)
    ang = jnp.arange(s_len).astype(jnp.float32)[:, None] * inv_freq[None, :]
    cos, sin = jnp.cos(ang), jnp.sin(ang)
    ones = jnp.ones((s_len, HEAD_DIM - ROPE_DIM), F32)
    zeros = jnp.zeros((s_len, HEAD_DIM - ROPE_DIM), F32)
    zh = jnp.zeros((s_len, half), F32)
    c_head = jnp.concatenate([cos, cos, ones], axis=1)
    s1_head = jnp.concatenate([-sin, zh, zeros], axis=1)
    s2_head = jnp.concatenate([zh, sin, zeros], axis=1)
    rep = lambda a: jnp.tile(a, (1, HEADS_PER_BLOCK))
    return rep(c_head), rep(s1_head), rep(s2_head)


def _bias_tables(s_len):
    i = np.arange(TQ)[None, :]
    jp = np.arange(s_len)[:, None]
    delta = i - jp + s_len - TQ
    causal = np.where(delta >= 0, 0.0, NEG).astype(np.float32)
    mult = np.zeros(delta.shape, np.float64)
    for window, dil in DILATION_PAIRS:
        mult += (delta >= 0) & (delta <= window) & (delta % dil == 0)
    dil_bias = np.where(mult > 0, np.log2(np.maximum(mult, 1.0)), NEG).astype(np.float32)
    return causal, dil_bias, float(np.log2(len(DILATION_PAIRS)))


def _resident(shape):
    nd = len(shape)
    return pl.BlockSpec(shape, lambda *_: (0,) * nd, pipeline_mode=pl.Buffered(1))


def _layer(x, g_mix, w_in, b_forget, g_q_fox, g_k_fox, g_q_dil, g_k_dil,
           g_out_fox, g_out_dil, w_out, g_ffn, w_gate, w_up, w_down):
    b_sz, s_len, d = x.shape
    n_heads = b_forget.shape[0]
    w = n_heads * HEAD_DIM
    d_ff = w_gate.shape[1]
    assert w_in.shape[1] == 6 * w + n_heads and w % MXU_DIM == 0
    assert s_len % TM_PROJ == 0 and s_len % TQ == 0 and (b_sz * s_len) % TM_FFN == 0
    assert d % LANES == 0 and n_heads <= AUG_STRIDE

    cols = np.cumsum([0, w, w, w, n_heads, w, w, w])
    w_main = jnp.concatenate([w_in[:, cols[i]:cols[i + 1]] for i in (0, 1, 2, 4, 5, 6)],
                             axis=1).astype(BF16)
    w_fa = jnp.zeros((d, LANES), F32).at[:, :n_heads].set(w_in[:, cols[3]:cols[4]]).astype(BF16)
    b_pad = jnp.zeros((1, LANES), F32).at[0, :n_heads].set(b_forget)
    scale = HEAD_DIM ** -0.5 * LOG2E
    gains = jnp.stack([jnp.tile(g_q_fox * scale, n_heads), jnp.tile(g_k_fox, n_heads),
                       jnp.tile(g_q_dil * scale, n_heads), jnp.tile(g_k_dil, n_heads)])
    head_id = np.arange(MXU_DIM) // HEAD_DIM
    bd = jnp.asarray((head_id[:, None] == head_id[None, :]) / HEAD_DIM, BF16)
    cos_t, s1_t, s2_t = _rope_tables(s_len)
    causal, dil_bias, dil_bias_max = _bias_tables(s_len)

    def logit_bound(gq, gk):
        return (HEAD_DIM * scale * BOUND_SLACK) * jnp.max(jnp.abs(gq)) * jnp.max(jnp.abs(gk))
    bound_fox = logit_bound(g_q_fox, g_k_fox)
    bound_dil = logit_bound(g_q_dil, g_k_dil)
    bounded_ok = 2.0 * jnp.maximum(bound_fox, bound_dil) + dil_bias_max <= EXP2_RANGE
    bnd_pad = jnp.zeros((1, LANES), F32).at[0, :n_heads].set(bound_fox)

    tok = lambda width: pl.BlockSpec((None, TM_PROJ, width), lambda b, t: (b, t, 0))
    pos = pl.BlockSpec((TM_PROJ, LANES), lambda b, t: (t, 0))
    bf = lambda width: jax.ShapeDtypeStruct((b_sz, s_len, width), BF16)
    qqf, kkf, vf, qd, kd, vd = pl.pallas_call(
        functools.partial(_proj_kernel, n_heads=n_heads),
        grid=(b_sz, s_len // TM_PROJ),
        in_specs=[tok(d), _resident((1, d)), _resident((d, 6 * w)), _resident((d, LANES)),
                  _resident((1, LANES)), _resident((1, LANES)), _resident((4, w)),
                  _resident((MXU_DIM, MXU_DIM)), pos, pos, pos],
        out_specs=[tok(2 * w), tok(2 * w), tok(w), tok(w), tok(w), tok(w)],
        out_shape=[bf(2 * w), bf(2 * w), bf(w), bf(w), bf(w), bf(w)],
        scratch_shapes=[pltpu.VMEM((8, LANES), F32)],
        compiler_params=pltpu.CompilerParams(
            dimension_semantics=("arbitrary", "arbitrary"), vmem_limit_bytes=VMEM_LIMIT),
        name="proj",
    )(x, g_mix.reshape(1, d), w_main, w_fa, b_pad, bnd_pad, gains, bd, cos_t, s1_t, s2_t)

    def attention(body, scratch, pairs, qq, kk, v, bias, dense_bias, name):
        kw = qq.shape[-1] // (w // LANES) * pairs
        blk = lambda width: pl.BlockSpec((None, s_len, width), lambda b, p: (b, 0, p))
        return pl.pallas_call(
            functools.partial(body, dense_bias=dense_bias),
            grid=(b_sz, w // LANES // pairs),
            in_specs=[blk(kw), blk(kw), blk(pairs * LANES), _resident((s_len, TQ))],
            out_specs=blk(pairs * LANES),
            out_shape=bf(w),
            scratch_shapes=scratch,
            compiler_params=pltpu.CompilerParams(
                dimension_semantics=("arbitrary", "arbitrary"), vmem_limit_bytes=VMEM_LIMIT),
            name=name,
        )(qq, kk, v, bias)

    nq = HEADS_PER_BLOCK * TQ
    vt_rows = LANES + ONES_ROWS
    exact_scratch = [pltpu.VMEM((vt_rows, s_len), BF16)] + [pltpu.VMEM((s_len, nq), F32)] * 3 + [
        pltpu.VMEM((vt_rows, nq), F32)]
    bounded_scratch = [pltpu.VMEM((PAIRS_PER_STEP, LANES, s_len), BF16)] + [
        pltpu.VMEM((s_len, nq), BF16)] * 2

    def attend_bounded():
        return (attention(_attn_bounded_kernel, bounded_scratch, PAIRS_PER_STEP, qqf, kkf, vf,
                          jnp.asarray(causal), False, "attn_fox"),
                attention(_attn_bounded_kernel, bounded_scratch, PAIRS_PER_STEP, qd, kd, vd,
                          jnp.asarray(dil_bias) - (bound_dil + dil_bias_max), True, "attn_dil"))

    def attend_exact():
        return (attention(_attn_exact_kernel, exact_scratch, 1, qqf, kkf, vf,
                          jnp.asarray(causal), False, "attn_fox_exact"),
                attention(_attn_exact_kernel, exact_scratch, 1, qd, kd, vd,
                          jnp.asarray(dil_bias), True, "attn_dil_exact"))

    o_fox, o_dil = lax.cond(bounded_ok, attend_bounded, attend_exact)

    n_tok = b_sz * s_len
    rows = lambda width: pl.BlockSpec((TM_FFN, width), lambda t: (t, 0))
    out = pl.pallas_call(
        _ffn_kernel,
        grid=(n_tok // TM_FFN,),
        in_specs=[rows(d), rows(w), rows(w), _resident((1, w)), _resident((1, w)),
                  _resident((w, d)), _resident((w, d)), _resident((1, d)),
                  _resident((d, d_ff)), _resident((d, d_ff)), _resident((d_ff, d))],
        out_specs=rows(d),
        out_shape=jax.ShapeDtypeStruct((n_tok, d), x.dtype),
        compiler_params=pltpu.CompilerParams(
            dimension_semantics=("arbitrary",), vmem_limit_bytes=VMEM_LIMIT),
        name="ffn",
    )(x.reshape(n_tok, d), o_fox.reshape(n_tok, w), o_dil.reshape(n_tok, w),
      g_out_fox.reshape(1, w), g_out_dil.reshape(1, w),
      w_out[:w].astype(BF16), w_out[w:].astype(BF16), g_ffn.reshape(1, d),
      w_gate.astype(BF16), w_up.astype(BF16), w_down.astype(BF16))
    return out.reshape(b_sz, s_len, d)


def kernel(x, g_mix, w_in, b_forget, g_q_fox, g_k_fox, g_q_dil, g_k_dil, g_out_fox, g_out_dil,
           w_out, g_ffn, w_gate, w_up, w_down):
    for l in range(g_mix.shape[0]):
        x = _layer(x, g_mix[l], w_in[l], b_forget[l], g_q_fox[l], g_k_fox[l], g_q_dil[l],
                   g_k_dil[l], g_out_fox[l], g_out_dil[l], w_out[l], g_ffn[l], w_gate[l],
                   w_up[l], w_down[l])
    return x
```

```python
import functools
import math

import jax
import jax.numpy as jnp
import numpy as np
from jax import lax
from jax.experimental import pallas as pl
from jax.experimental.pallas import tpu as pltpu

F32 = jnp.float32
BF16 = jnp.bfloat16

HEAD_DIM = 64
ROPE_DIM = HEAD_DIM // 4
ROPE_THETA = 500000.0
DILATION_PAIRS = ((128, 1), (512, 4), (2048, 16))
EPS = 1e-6
NEG = -1e30
LOG2E = math.log2(math.e)

LANES = 128
SUBLANES = 8
MXU_DIM = 256
HEADS_PER_BLOCK = LANES // HEAD_DIM
ONES_ROWS = 16
AUG_TERMS = 3
AUG_STRIDE = 8
Q_ONES = 4 * AUG_STRIDE

TM_PROJ = 1024
PROJ_SPLIT = 2
TQ = 256
TM_FFN = 1024
FFN_SPLIT = 4
VMEM_LIMIT = 56 * 1024 * 1024
PV_LAG = 1
PAIRS_PER_STEP = 2
EXP2_RANGE = 64.0
BOUND_SLACK = 1.02


def _split3(v):
    hi = v.astype(BF16).astype(F32)
    r1 = v - hi
    mid = r1.astype(BF16).astype(F32)
    lo = (r1 - mid).astype(BF16).astype(F32)
    return hi, mid, lo


def _pack3(v):
    hi, mid, lo = _split3(v)
    return hi + pltpu.roll(mid, AUG_STRIDE, 1) + pltpu.roll(lo, 2 * AUG_STRIDE, 1)


def _round_robin(stage_generators):
    active, done = list(stage_generators), object()
    while active:
        active = [g for g in active if next(g, done) is not done]


def _proj_kernel(x_ref, gmix_ref, wmain_ref, wfa_ref, bf_ref, bnd_ref, gains_ref, bd_ref,
                 cos_ref, s1_ref, s2_ref,
                 qqf_ref, kkf_ref, vf_ref, qd_ref, kd_ref, vd_ref, carry_ref, *, n_heads):
    t = pl.program_id(1)
    tm, d = x_ref.shape
    rows = tm // PROJ_SPLIT
    w = n_heads * HEAD_DIM
    n_blk = w // LANES
    lane = lax.broadcasted_iota(jnp.int32, (rows, LANES), 1)
    row = lax.broadcasted_iota(jnp.int32, (rows, LANES), 0)
    term_lane = (lane % AUG_STRIDE < HEADS_PER_BLOCK) & (lane < AUG_TERMS * AUG_STRIDE)
    ones_lane = ((lane % AUG_STRIDE < HEADS_PER_BLOCK) & (lane >= Q_ONES)
                 & (lane < Q_ONES + AUG_TERMS * AUG_STRIDE))

    @pl.when(t == 0)
    def _():
        carry_ref[...] = jnp.zeros_like(carry_ref)

    def mean_squares(p):
        sq = (p * p).astype(BF16)
        parts = [jnp.dot(sq[:, j * MXU_DIM:(j + 1) * MXU_DIM], bd_ref[...],
                         preferred_element_type=F32) for j in range(w // MXU_DIM)]
        return jnp.concatenate(parts, axis=1)

    def rope(xc, rs):
        return (xc * cos_ref[rs, :] + pltpu.roll(xc, LANES - ROPE_DIM // 2, 1) * s1_ref[rs, :]
                + pltpu.roll(xc, ROPE_DIM // 2, 1) * s2_ref[rs, :])

    def row_group(rs):
        x = x_ref[rs, :]
        ms = jnp.mean(x * x, axis=-1, keepdims=True)
        h = (x * lax.rsqrt(ms + EPS) * gmix_ref[...]).astype(BF16)

        def group(i):
            return jnp.dot(h, wmain_ref[:, i * w:(i + 1) * w], preferred_element_type=F32)

        half = d // 2
        fa = (jnp.dot(h[:, :half], wfa_ref[:half, :], preferred_element_type=F32)
              + jnp.dot(h[:, half:], wfa_ref[half:, :], preferred_element_type=F32)
              + bf_ref[...])
        g_q = group(0)
        yield

        logf = jnp.minimum(fa, 0.0) - jnp.log1p(jnp.exp(-jnp.abs(fa)))
        logf = jnp.where(lane < n_heads, logf * LOG2E, 0.0)
        g_k = group(1)
        yield

        ctile = logf
        shift = 1
        while shift < rows:
            ctile = ctile + jnp.where(row >= shift, pltpu.roll(ctile, shift, 0), 0.0)
            shift *= 2
        c = jnp.where(lane < n_heads, ctile + carry_ref[0:1, :], 0.0)
        carry_ref[...] = jnp.broadcast_to(c[rows - 1:rows, :], carry_ref.shape)
        cq = _pack3(c)
        ck = _pack3(jnp.where(lane < n_heads, c + bnd_ref[...], 0.0))

        def gate_lanes(p):
            shift = HEADS_PER_BLOCK * p
            qa = jnp.where(term_lane, pltpu.roll(cq, (LANES - shift) % LANES, 1),
                           jnp.where(ones_lane, 1.0, 0.0))
            ka = jnp.where(ones_lane, -pltpu.roll(ck, (Q_ONES - shift) % LANES, 1),
                           jnp.where(term_lane, 1.0, 0.0))
            return qa.astype(BF16), ka.astype(BF16)

        ms_q = mean_squares(g_q)
        g_v = group(2)
        yield
        qf = g_q * lax.rsqrt(ms_q + EPS) * gains_ref[0:1, :]
        ms_k = mean_squares(g_k)
        vf_ref[rs, :] = g_v.astype(BF16)
        g_qd = group(3)
        yield
        kf = g_k * lax.rsqrt(ms_k + EPS) * gains_ref[1:2, :]
        for j in range(n_blk):
            qa, ka = gate_lanes(j)
            qqf_ref[rs, 2 * j * LANES:(2 * j + 1) * LANES] = qf[:, j * LANES:(j + 1) * LANES].astype(BF16)
            qqf_ref[rs, (2 * j + 1) * LANES:(2 * j + 2) * LANES] = qa
            kkf_ref[rs, 2 * j * LANES:(2 * j + 1) * LANES] = kf[:, j * LANES:(j + 1) * LANES].astype(BF16)
            kkf_ref[rs, (2 * j + 1) * LANES:(2 * j + 2) * LANES] = ka

        ms_qd = mean_squares(g_qd)
        g_kd = group(4)
        yield
        qd = g_qd * lax.rsqrt(ms_qd + EPS) * gains_ref[2:3, :]
        ms_kd = mean_squares(g_kd)
        g_vd = group(5)
        yield
        kd = g_kd * lax.rsqrt(ms_kd + EPS) * gains_ref[3:4, :]
        for j in range(n_blk):
            qd_ref[rs, j * LANES:(j + 1) * LANES] = rope(qd[:, j * LANES:(j + 1) * LANES], rs).astype(BF16)
            kd_ref[rs, j * LANES:(j + 1) * LANES] = rope(kd[:, j * LANES:(j + 1) * LANES], rs).astype(BF16)
        vd_ref[rs, :] = g_vd.astype(BF16)

    _round_robin(row_group(slice(r * rows, (r + 1) * rows)) for r in range(PROJ_SPLIT))


def _attn_exact_kernel(qq_ref, kk_ref, v_ref, g_ref, o_ref, vt_scr, s_scr0, s_scr1, s_scr2,
                       acc_scr, *, dense_bias):
    s_len, kw = qq_ref.shape
    n_units = s_len // TQ
    lane_q = lax.broadcasted_iota(jnp.int32, (TQ, kw), 1)
    vt_scr[0:LANES, :] = v_ref[...].T
    vt_scr[LANES:, :] = jnp.ones((ONES_ROWS, s_len), BF16)

    q_masks = []
    for j in range(HEADS_PER_BLOCK):
        m = (lane_q >= j * HEAD_DIM) & (lane_q < (j + 1) * HEAD_DIM)
        if kw > LANES:
            m = m | ((lane_q >= LANES) & (lane_q % AUG_STRIDE == j))
        q_masks.append(m)

    s_scr = (s_scr0, s_scr1, s_scr2)
    col_max = [None] * n_units

    def scores(i, c, qcat):
        s = lax.dot_general(kk_ref[c * TQ:(c + 1) * TQ, :], qcat, (((1,), (1,)), ((), ())),
                            preferred_element_type=F32)
        if dense_bias or c == i:
            r0 = s_len - (i + 1 - c) * TQ
            g = g_ref[r0:r0 + TQ, :]
            s = s + jnp.concatenate([g] * HEADS_PER_BLOCK, axis=1)
        s_scr[i % len(s_scr)][c * TQ:(c + 1) * TQ, :] = s
        cm = jnp.max(s, axis=0, keepdims=True)
        col_max[i] = cm if col_max[i] is None else jnp.maximum(col_max[i], cm)

    def probs(i, c):
        return jnp.exp2((s_scr[i % len(s_scr)][c * TQ:(c + 1) * TQ, :] - col_max[i]).astype(BF16))

    def weighted_values(c, p, first):
        o = jnp.dot(vt_scr[:, c * TQ:(c + 1) * TQ], p, preferred_element_type=F32)
        acc_scr[...] = o if first else acc_scr[...] + o

    def score_steps(i):
        qb = qq_ref[i * TQ:(i + 1) * TQ, :]
        qcat = jnp.concatenate([jnp.where(q_masks[j], qb, jnp.zeros((), BF16))
                                for j in range(HEADS_PER_BLOCK)], axis=0)
        return [functools.partial(scores, i, c, qcat) for c in range(i + 1)]

    def finish(i):
        res = []
        for j in range(HEADS_PER_BLOCK):
            num = acc_scr[j * HEAD_DIM:(j + 1) * HEAD_DIM, j * TQ:(j + 1) * TQ]
            den = acc_scr[LANES:LANES + 1, j * TQ:(j + 1) * TQ]
            res.append(num / den)
        o_ref[i * TQ:(i + 1) * TQ, :] = jnp.concatenate(res, axis=0).T.astype(o_ref.dtype)
        col_max[i] = None

    lead = len(s_scr) - 1
    for u in range(min(lead, n_units)):
        for step in score_steps(u):
            step()
    for i in range(n_units):
        n_chunks = i + 1
        nxt = score_steps(i + lead) if i + lead < n_units else []
        p_prev = None
        for k in range(max(n_chunks + 1, len(nxt))):
            if k < len(nxt):
                nxt[k]()
            if p_prev is not None:
                weighted_values(k - 1, p_prev, k == 1)
            p_prev = probs(i, k) if k < n_chunks else None
        finish(i)


def _attn_bounded_kernel(qq_ref, kk_ref, v_ref, g_ref, o_ref, vt_scr, p_scr0, p_scr1,
                         *qt_scr, dense_bias):
    s_len = qq_ref.shape[0]
    n_pairs = v_ref.shape[1] // LANES
    kw = qq_ref.shape[1] // n_pairs
    n_blocks = s_len // TQ
    lane_q = lax.broadcasted_iota(jnp.int32, (TQ, kw), 1)
    for pr in range(n_pairs):
        vt_scr[pr] = v_ref[:, pr * LANES:(pr + 1) * LANES].T

    q_masks = []
    for j in range(HEADS_PER_BLOCK):
        m = (lane_q >= j * HEAD_DIM) & (lane_q < (j + 1) * HEAD_DIM)
        if kw > LANES:
            m = m | ((lane_q >= LANES) & (lane_q % AUG_STRIDE == j))
        q_masks.append(m)
    p_scr = (p_scr0, p_scr1)
    units = [(pr, i) for pr in range(n_pairs) for i in range(n_blocks)]
    col_sum = [None] * len(units)

    def chunk(n, c, qcat):
        pr, i = units[n]
        keys = kk_ref[c * TQ:(c + 1) * TQ, pr * kw:(pr + 1) * kw]
        if qt_scr:
            s = jnp.dot(keys, qcat[...], preferred_element_type=F32)
        else:
            s = lax.dot_general(keys, qcat, (((1,), (1,)), ((), ())),
                                preferred_element_type=F32)
        if dense_bias or c == i:
            r0 = s_len - (i + 1 - c) * TQ
            g = g_ref[r0:r0 + TQ, :]
            s = s + jnp.concatenate([g] * HEADS_PER_BLOCK, axis=1)
        p = jnp.exp2(s)
        part = jnp.sum(p.reshape(TQ // SUBLANES, SUBLANES, p.shape[1]), axis=0)
        col_sum[n] = part if col_sum[n] is None else col_sum[n] + part
        p_scr[n % len(p_scr)][c * TQ:(c + 1) * TQ, :] = p.astype(BF16)

    def unit_steps(n):
        pr, i = units[n]
        qb = qq_ref[i * TQ:(i + 1) * TQ, pr * kw:(pr + 1) * kw]
        qcat = jnp.concatenate([jnp.where(q_masks[j], qb, jnp.zeros((), BF16))
                                for j in range(HEADS_PER_BLOCK)], axis=0)
        if qt_scr:
            qcat_t = qt_scr[n % len(qt_scr)]
            qcat_t[...] = qcat.T
            qcat = qcat_t
        return [functools.partial(chunk, n, c, qcat) for c in range(i + 1)]

    def finish(n):
        pr, i = units[n]
        q1 = (i + 1) * TQ
        o = jnp.dot(vt_scr[pr, :, 0:q1], p_scr[n % len(p_scr)][0:q1, :],
                    preferred_element_type=F32)
        sums = jnp.sum(col_sum[n], axis=0, keepdims=True)
        col_sum[n] = None
        res = []
        for j in range(HEADS_PER_BLOCK):
            num = o[j * HEAD_DIM:(j + 1) * HEAD_DIM, j * TQ:(j + 1) * TQ]
            res.append(num / sums[:, j * TQ:(j + 1) * TQ])
        o_ref[i * TQ:(i + 1) * TQ, pr * LANES:(pr + 1) * LANES] = (
            jnp.concatenate(res, axis=0).T.astype(o_ref.dtype))

    for n in range(len(units)):
        steps = unit_steps(n)
        pv_at = min(PV_LAG, len(steps) - 1)
        for k, step in enumerate(steps):
            step()
            if n > 0 and k == pv_at:
                finish(n - 1)
    finish(len(units) - 1)


def _staggered(stage_generators):
    waiting, active, done = list(stage_generators), [], object()
    while waiting or active:
        if waiting:
            active.append(waiting.pop(0))
        active = [g for g in active if next(g, done) is not done]


def _ffn_kernel(x_ref, of_ref, od_ref, gof_ref, god_ref, wof_ref, wod_ref, gffn_ref,
                wg_ref, wu_ref, wd_ref, out_ref):
    def norm(v, g):
        return v * lax.rsqrt(jnp.mean(v * v, axis=-1, keepdims=True) + EPS) * g

    rows = x_ref.shape[0] // FFN_SPLIT

    def row_group(rs):
        nf = norm(of_ref[rs, :].astype(F32), gof_ref[...]).astype(BF16)
        nd = norm(od_ref[rs, :].astype(F32), god_ref[...]).astype(BF16)
        x1 = (x_ref[rs, :] + jnp.dot(nf, wof_ref[...], preferred_element_type=F32)
              + jnp.dot(nd, wod_ref[...], preferred_element_type=F32))
        yield
        h = norm(x1, gffn_ref[...]).astype(BF16)
        a = jnp.dot(h, wg_ref[...], preferred_element_type=F32)
        u = jnp.dot(h, wu_ref[...], preferred_element_type=F32)
        yield
        g = (a * (1.0 / (1.0 + jnp.exp(-a))) * u).astype(BF16)
        out_ref[rs, :] = x1 + jnp.dot(g, wd_ref[...], preferred_element_type=F32)

    _staggered(row_group(slice(r * rows, (r + 1) * rows)) for r in range(FFN_SPLIT))


def _rope_tables(s_len):
    half = ROPE_DIM // 2
    inv_freq = jnp.power(jnp.float32(ROPE_THETA),
                         -jnp.arange(half, dtype=jnp.float32) * 2.0 / ROPE_DIM)
    ang = jnp.arange(s_len).astype(jnp.float32)[:, None] * inv_freq[None, :]
    cos, sin = jnp.cos(ang), jnp.sin(ang)
    ones = jnp.ones((s_len, HEAD_DIM - ROPE_DIM), F32)
    zeros = jnp.zeros((s_len, HEAD_DIM - ROPE_DIM), F32)
    zh = jnp.zeros((s_len, half), F32)
    c_head = jnp.concatenate([cos, cos, ones], axis=1)
    s1_head = jnp.concatenate([-sin, zh, zeros], axis=1)
    s2_head = jnp.concatenate([zh, sin, zeros], axis=1)
    rep = lambda a: jnp.tile(a, (1, HEADS_PER_BLOCK))
    return rep(c_head), rep(s1_head), rep(s2_head)


def _bias_tables(s_len):
    i = np.arange(TQ)[None, :]
    jp = np.arange(s_len)[:, None]
    delta = i - jp + s_len - TQ
    causal = np.where(delta >= 0, 0.0, NEG).astype(np.float32)
    mult = np.zeros(delta.shape, np.float64)
    for window, dil in DILATION_PAIRS:
        mult += (delta >= 0) & (delta <= window) & (delta % dil == 0)
    dil_bias = np.where(mult > 0, np.log2(np.maximum(mult, 1.0)), NEG).astype(np.float32)
    return causal, dil_bias, float(np.log2(len(DILATION_PAIRS)))


def _resident(shape):
    nd = len(shape)
    return pl.BlockSpec(shape, lambda *_: (0,) * nd, pipeline_mode=pl.Buffered(1))


def _layer(x, g_mix, w_in, b_forget, g_q_fox, g_k_fox, g_q_dil, g_k_dil,
           g_out_fox, g_out_dil, w_out, g_ffn, w_gate, w_up, w_down):
    b_sz, s_len, d = x.shape
    n_heads = b_forget.shape[0]
    w = n_heads * HEAD_DIM
    d_ff = w_gate.shape[1]
    assert w_in.shape[1] == 6 * w + n_heads and w % MXU_DIM == 0
    assert s_len % TM_PROJ == 0 and s_len % TQ == 0 and (b_sz * s_len) % TM_FFN == 0
    assert d % LANES == 0 and n_heads <= AUG_STRIDE

    cols = np.cumsum([0, w, w, w, n_heads, w, w, w])
    w_main = jnp.concatenate([w_in[:, cols[i]:cols[i + 1]] for i in (0, 1, 2, 4, 5, 6)],
                             axis=1).astype(BF16)
    w_fa = jnp.zeros((d, LANES), F32).at[:, :n_heads].set(w_in[:, cols[3]:cols[4]]).astype(BF16)
    b_pad = jnp.zeros((1, LANES), F32).at[0, :n_heads].set(b_forget)
    scale = HEAD_DIM ** -0.5 * LOG2E
    gains = jnp.stack([jnp.tile(g_q_fox * scale, n_heads), jnp.tile(g_k_fox, n_heads),
                       jnp.tile(g_q_dil * scale, n_heads), jnp.tile(g_k_dil, n_heads)])
    head_id = np.arange(MXU_DIM) // HEAD_DIM
    bd = jnp.asarray((head_id[:, None] == head_id[None, :]) / HEAD_DIM, BF16)
    cos_t, s1_t, s2_t = _rope_tables(s_len)
    causal, dil_bias, dil_bias_max = _bias_tables(s_len)

    def logit_bound(gq, gk):
        return (HEAD_DIM * scale * BOUND_SLACK) * jnp.max(jnp.abs(gq)) * jnp.max(jnp.abs(gk))
    bound_fox = logit_bound(g_q_fox, g_k_fox)
    bound_dil = logit_bound(g_q_dil, g_k_dil)
    bounded_ok = 2.0 * jnp.maximum(bound_fox, bound_dil) + dil_bias_max <= EXP2_RANGE
    bnd_pad = jnp.zeros((1, LANES), F32).at[0, :n_heads].set(bound_fox)

    tok = lambda width: pl.BlockSpec((None, TM_PROJ, width), lambda b, t: (b, t, 0))
    pos = pl.BlockSpec((TM_PROJ, LANES), lambda b, t: (t, 0))
    bf = lambda width: jax.ShapeDtypeStruct((b_sz, s_len, width), BF16)
    qqf, kkf, vf, qd, kd, vd = pl.pallas_call(
        functools.partial(_proj_kernel, n_heads=n_heads),
        grid=(b_sz, s_len // TM_PROJ),
        in_specs=[tok(d), _resident((1, d)), _resident((d, 6 * w)), _resident((d, LANES)),
                  _resident((1, LANES)), _resident((1, LANES)), _resident((4, w)),
                  _resident((MXU_DIM, MXU_DIM)), pos, pos, pos],
        out_specs=[tok(2 * w), tok(2 * w), tok(w), tok(w), tok(w), tok(w)],
        out_shape=[bf(2 * w), bf(2 * w), bf(w), bf(w), bf(w), bf(w)],
        scratch_shapes=[pltpu.VMEM((8, LANES), F32)],
        compiler_params=pltpu.CompilerParams(
            dimension_semantics=("arbitrary", "arbitrary"), vmem_limit_bytes=VMEM_LIMIT),
        name="proj",
    )(x, g_mix.reshape(1, d), w_main, w_fa, b_pad, bnd_pad, gains, bd, cos_t, s1_t, s2_t)

    def attention(body, scratch, pairs, qq, kk, v, bias, dense_bias, name):
        kw = qq.shape[-1] // (w // LANES) * pairs
        blk = lambda width: pl.BlockSpec((None, s_len, width), lambda b, p: (b, 0, p))
        return pl.pallas_call(
            functools.partial(body, dense_bias=dense_bias),
            grid=(b_sz, w // LANES // pairs),
            in_specs=[blk(kw), blk(kw), blk(pairs * LANES), _resident((s_len, TQ))],
            out_specs=blk(pairs * LANES),
            out_shape=bf(w),
            scratch_shapes=scratch,
            compiler_params=pltpu.CompilerParams(
                dimension_semantics=("arbitrary", "arbitrary"), vmem_limit_bytes=VMEM_LIMIT),
            name=name,
        )(qq, kk, v, bias)

    nq = HEADS_PER_BLOCK * TQ
    vt_rows = LANES + ONES_ROWS
    exact_scratch = [pltpu.VMEM((vt_rows, s_len), BF16)] + [pltpu.VMEM((s_len, nq), F32)] * 3 + [
        pltpu.VMEM((vt_rows, nq), F32)]
    bounded_scratch = [pltpu.VMEM((PAIRS_PER_STEP, LANES, s_len), BF16)] + [
        pltpu.VMEM((s_len, nq), BF16)] * 2

    fox_scratch = bounded_scratch + [pltpu.VMEM((2 * LANES, nq), BF16)] * 2

    def attend_bounded():
        return (attention(_attn_bounded_kernel, fox_scratch, PAIRS_PER_STEP, qqf, kkf, vf,
                          jnp.asarray(causal), False, "attn_fox"),
                attention(_attn_bounded_kernel, bounded_scratch, PAIRS_PER_STEP, qd, kd, vd,
                          jnp.asarray(dil_bias) - (bound_dil + dil_bias_max), True, "attn_dil"))

    def attend_exact():
        return (attention(_attn_exact_kernel, exact_scratch, 1, qqf, kkf, vf,
                          jnp.asarray(causal), False, "attn_fox_exact"),
                attention(_attn_exact_kernel, exact_scratch, 1, qd, kd, vd,
                          jnp.asarray(dil_bias), True, "attn_dil_exact"))

    o_fox, o_dil = lax.cond(bounded_ok, attend_bounded, attend_exact)

    n_tok = b_sz * s_len
    rows = lambda width: pl.BlockSpec((TM_FFN, width), lambda t: (t, 0))
    out = pl.pallas_call(
        _ffn_kernel,
        grid=(n_tok // TM_FFN,),
        in_specs=[rows(d), rows(w), rows(w), _resident((1, w)), _resident((1, w)),
                  _resident((w, d)), _resident((w, d)), _resident((1, d)),
                  _resident((d, d_ff)), _resident((d, d_ff)), _resident((d_ff, d))],
        out_specs=rows(d),
        out_shape=jax.ShapeDtypeStruct((n_tok, d), x.dtype),
        compiler_params=pltpu.CompilerParams(
            dimension_semantics=("arbitrary",), vmem_limit_bytes=VMEM_LIMIT),
        name="ffn",
    )(x.reshape(n_tok, d), o_fox.reshape(n_tok, w), o_dil.reshape(n_tok, w),
      g_out_fox.reshape(1, w), g_out_dil.reshape(1, w),
      w_out[:w].astype(BF16), w_out[w:].astype(BF16), g_ffn.reshape(1, d),
      w_gate.astype(BF16), w_up.astype(BF16), w_down.astype(BF16))
    return out.reshape(b_sz, s_len, d)


def kernel(x, g_mix, w_in, b_forget, g_q_fox, g_k_fox, g_q_dil, g_k_dil, g_out_fox, g_out_dil,
           w_out, g_ffn, w_gate, w_up, w_down):
    for l in range(g_mix.shape[0]):
        x = _layer(x, g_mix[l], w_in[l], b_forget[l], g_q_fox[l], g_k_fox[l], g_q_dil[l],
                   g_k_dil[l], g_out_fox[l], g_out_dil[l], w_out[l], g_ffn[l], w_gate[l],
                   w_up[l], w_down[l])
    return x
```

```python
import functools
import math

import jax
import jax.numpy as jnp
import numpy as np
from jax import lax
from jax.experimental import pallas as pl
from jax.experimental.pallas import tpu as pltpu

F32 = jnp.float32
BF16 = jnp.bfloat16

HEAD_DIM = 64
ROPE_DIM = HEAD_DIM // 4
ROPE_THETA = 500000.0
DILATION_PAIRS = ((128, 1), (512, 4), (2048, 16))
EPS = 1e-6
NEG = -1e30
LOG2E = math.log2(math.e)

LANES = 128
SUBLANES = 8
MXU_DIM = 256
HEADS_PER_BLOCK = LANES // HEAD_DIM
ONES_ROWS = 16
AUG_TERMS = 3
AUG_STRIDE = 8
Q_ONES = 4 * AUG_STRIDE

TM_PROJ = 1024
PROJ_SPLIT = 2
TQ = 256
TM_FFN = 1024
FFN_SPLIT = 4
VMEM_LIMIT = 56 * 1024 * 1024
PV_LAG = 1
PAIRS_PER_STEP = 2
EXP2_RANGE = 64.0
BOUND_SLACK = 1.02


def _split3(v):
    hi = v.astype(BF16).astype(F32)
    r1 = v - hi
    mid = r1.astype(BF16).astype(F32)
    lo = (r1 - mid).astype(BF16).astype(F32)
    return hi, mid, lo


def _pack3(v):
    hi, mid, lo = _split3(v)
    return hi + pltpu.roll(mid, AUG_STRIDE, 1) + pltpu.roll(lo, 2 * AUG_STRIDE, 1)


def _round_robin(stage_generators):
    active, done = list(stage_generators), object()
    while active:
        active = [g for g in active if next(g, done) is not done]


def _proj_kernel(x_ref, gmix_ref, wfox_ref, wdil_ref, wfa_ref, bf_ref, bnd_ref, gains_ref, bd_ref,
                 cos_ref, s1_ref, s2_ref,
                 qqf_ref, kkf_ref, vf_ref, qd_ref, kd_ref, vd_ref, carry_ref, *, n_heads):
    t = pl.program_id(1)
    tm, d = x_ref.shape
    rows = tm // PROJ_SPLIT
    w = n_heads * HEAD_DIM
    n_blk = w // LANES
    lane = lax.broadcasted_iota(jnp.int32, (rows, LANES), 1)
    row = lax.broadcasted_iota(jnp.int32, (rows, LANES), 0)
    term_lane = (lane % AUG_STRIDE < HEADS_PER_BLOCK) & (lane < AUG_TERMS * AUG_STRIDE)
    ones_lane = ((lane % AUG_STRIDE < HEADS_PER_BLOCK) & (lane >= Q_ONES)
                 & (lane < Q_ONES + AUG_TERMS * AUG_STRIDE))

    @pl.when(t == 0)
    def _():
        carry_ref[...] = jnp.zeros_like(carry_ref)

    def mean_squares(p):
        sq = (p * p).astype(BF16)
        parts = [jnp.dot(sq[:, j * MXU_DIM:(j + 1) * MXU_DIM], bd_ref[...],
                         preferred_element_type=F32) for j in range(w // MXU_DIM)]
        return jnp.concatenate(parts, axis=1)

    def rope(xc, rs):
        return (xc * cos_ref[rs, :] + pltpu.roll(xc, LANES - ROPE_DIM // 2, 1) * s1_ref[rs, :]
                + pltpu.roll(xc, ROPE_DIM // 2, 1) * s2_ref[rs, :])

    def row_group(rs):
        x = x_ref[rs, :]
        ms = jnp.mean(x * x, axis=-1, keepdims=True)
        h = (x * lax.rsqrt(ms + EPS) * gmix_ref[...]).astype(BF16)

        def group(i):
            w_ref, k = (wfox_ref, i) if i < 3 else (wdil_ref, i - 3)
            return jnp.dot(h, w_ref[:, k * w:(k + 1) * w], preferred_element_type=F32)

        half = d // 2
        fa = (jnp.dot(h[:, :half], wfa_ref[:half, :], preferred_element_type=F32)
              + jnp.dot(h[:, half:], wfa_ref[half:, :], preferred_element_type=F32)
              + bf_ref[...])
        g_q = group(0)
        yield

        logf = jnp.minimum(fa, 0.0) - jnp.log1p(jnp.exp(-jnp.abs(fa)))
        logf = jnp.where(lane < n_heads, logf * LOG2E, 0.0)
        g_k = group(1)
        yield

        ctile = logf
        shift = 1
        while shift < rows:
            ctile = ctile + jnp.where(row >= shift, pltpu.roll(ctile, shift, 0), 0.0)
            shift *= 2
        c = jnp.where(lane < n_heads, ctile + carry_ref[0:1, :], 0.0)
        carry_ref[...] = jnp.broadcast_to(c[rows - 1:rows, :], carry_ref.shape)
        cq = _pack3(c)
        ck = _pack3(jnp.where(lane < n_heads, c + bnd_ref[...], 0.0))

        def gate_lanes(p):
            shift = HEADS_PER_BLOCK * p
            qa = jnp.where(term_lane, pltpu.roll(cq, (LANES - shift) % LANES, 1),
                           jnp.where(ones_lane, 1.0, 0.0))
            ka = jnp.where(ones_lane, -pltpu.roll(ck, (Q_ONES - shift) % LANES, 1),
                           jnp.where(term_lane, 1.0, 0.0))
            return qa.astype(BF16), ka.astype(BF16)

        ms_q = mean_squares(g_q)
        g_v = group(2)
        yield
        qf = g_q * lax.rsqrt(ms_q + EPS) * gains_ref[0:1, :]
        ms_k = mean_squares(g_k)
        vf_ref[rs, :] = g_v.astype(BF16)
        g_qd = group(3)
        yield
        kf = g_k * lax.rsqrt(ms_k + EPS) * gains_ref[1:2, :]
        for j in range(n_blk):
            qa, ka = gate_lanes(j)
            qqf_ref[rs, 2 * j * LANES:(2 * j + 1) * LANES] = qf[:, j * LANES:(j + 1) * LANES].astype(BF16)
            qqf_ref[rs, (2 * j + 1) * LANES:(2 * j + 2) * LANES] = qa
            kkf_ref[rs, 2 * j * LANES:(2 * j + 1) * LANES] = kf[:, j * LANES:(j + 1) * LANES].astype(BF16)
            kkf_ref[rs, (2 * j + 1) * LANES:(2 * j + 2) * LANES] = ka

        ms_qd = mean_squares(g_qd)
        g_kd = group(4)
        yield
        qd = g_qd * lax.rsqrt(ms_qd + EPS) * gains_ref[2:3, :]
        ms_kd = mean_squares(g_kd)
        g_vd = group(5)
        yield
        kd = g_kd * lax.rsqrt(ms_kd + EPS) * gains_ref[3:4, :]
        for j in range(n_blk):
            qd_ref[rs, j * LANES:(j + 1) * LANES] = rope(qd[:, j * LANES:(j + 1) * LANES], rs).astype(BF16)
            kd_ref[rs, j * LANES:(j + 1) * LANES] = rope(kd[:, j * LANES:(j + 1) * LANES], rs).astype(BF16)
        vd_ref[rs, :] = g_vd.astype(BF16)

    _round_robin(row_group(slice(r * rows, (r + 1) * rows)) for r in range(PROJ_SPLIT))


def _attn_exact_kernel(qq_ref, kk_ref, v_ref, g_ref, o_ref, vt_scr, s_scr0, s_scr1, s_scr2,
                       acc_scr, *, dense_bias):
    s_len, kw = qq_ref.shape
    n_units = s_len // TQ
    lane_q = lax.broadcasted_iota(jnp.int32, (TQ, kw), 1)
    vt_scr[0:LANES, :] = v_ref[...].T
    vt_scr[LANES:, :] = jnp.ones((ONES_ROWS, s_len), BF16)

    q_masks = []
    for j in range(HEADS_PER_BLOCK):
        m = (lane_q >= j * HEAD_DIM) & (lane_q < (j + 1) * HEAD_DIM)
        if kw > LANES:
            m = m | ((lane_q >= LANES) & (lane_q % AUG_STRIDE == j))
        q_masks.append(m)

    s_scr = (s_scr0, s_scr1, s_scr2)
    col_max = [None] * n_units

    def scores(i, c, qcat):
        s = lax.dot_general(kk_ref[c * TQ:(c + 1) * TQ, :], qcat, (((1,), (1,)), ((), ())),
                            preferred_element_type=F32)
        if dense_bias or c == i:
            r0 = s_len - (i + 1 - c) * TQ
            g = g_ref[r0:r0 + TQ, :]
            s = s + jnp.concatenate([g] * HEADS_PER_BLOCK, axis=1)
        s_scr[i % len(s_scr)][c * TQ:(c + 1) * TQ, :] = s
        cm = jnp.max(s, axis=0, keepdims=True)
        col_max[i] = cm if col_max[i] is None else jnp.maximum(col_max[i], cm)

    def probs(i, c):
        return jnp.exp2((s_scr[i % len(s_scr)][c * TQ:(c + 1) * TQ, :] - col_max[i]).astype(BF16))

    def weighted_values(c, p, first):
        o = jnp.dot(vt_scr[:, c * TQ:(c + 1) * TQ], p, preferred_element_type=F32)
        acc_scr[...] = o if first else acc_scr[...] + o

    def score_steps(i):
        qb = qq_ref[i * TQ:(i + 1) * TQ, :]
        qcat = jnp.concatenate([jnp.where(q_masks[j], qb, jnp.zeros((), BF16))
                                for j in range(HEADS_PER_BLOCK)], axis=0)
        return [functools.partial(scores, i, c, qcat) for c in range(i + 1)]

    def finish(i):
        res = []
        for j in range(HEADS_PER_BLOCK):
            num = acc_scr[j * HEAD_DIM:(j + 1) * HEAD_DIM, j * TQ:(j + 1) * TQ]
            den = acc_scr[LANES:LANES + 1, j * TQ:(j + 1) * TQ]
            res.append(num / den)
        o_ref[i * TQ:(i + 1) * TQ, :] = jnp.concatenate(res, axis=0).T.astype(o_ref.dtype)
        col_max[i] = None

    lead = len(s_scr) - 1
    for u in range(min(lead, n_units)):
        for step in score_steps(u):
            step()
    for i in range(n_units):
        n_chunks = i + 1
        nxt = score_steps(i + lead) if i + lead < n_units else []
        p_prev = None
        for k in range(max(n_chunks + 1, len(nxt))):
            if k < len(nxt):
                nxt[k]()
            if p_prev is not None:
                weighted_values(k - 1, p_prev, k == 1)
            p_prev = probs(i, k) if k < n_chunks else None
        finish(i)


def _attn_bounded_kernel(qq_ref, kk_ref, v_ref, g_ref, o_ref, vt_scr, p_scr0, p_scr1,
                         *qt_scr, dense_bias):
    s_len = qq_ref.shape[0]
    n_pairs = v_ref.shape[1] // LANES
    kw = qq_ref.shape[1] // n_pairs
    n_blocks = s_len // TQ
    lane_q = lax.broadcasted_iota(jnp.int32, (TQ, kw), 1)
    for pr in range(n_pairs):
        vt_scr[pr] = v_ref[:, pr * LANES:(pr + 1) * LANES].T

    q_masks = []
    for j in range(HEADS_PER_BLOCK):
        m = (lane_q >= j * HEAD_DIM) & (lane_q < (j + 1) * HEAD_DIM)
        if kw > LANES:
            m = m | ((lane_q >= LANES) & (lane_q % AUG_STRIDE == j))
        q_masks.append(m)
    p_scr = (p_scr0, p_scr1)
    units = [(pr, i) for pr in range(n_pairs) for i in range(n_blocks)]
    col_sum = [None] * len(units)

    def chunk(n, c, qcat):
        pr, i = units[n]
        keys = kk_ref[c * TQ:(c + 1) * TQ, pr * kw:(pr + 1) * kw]
        if qt_scr:
            s = jnp.dot(keys, qcat[...], preferred_element_type=F32)
        else:
            s = lax.dot_general(keys, qcat, (((1,), (1,)), ((), ())),
                                preferred_element_type=F32)
        if dense_bias or c == i:
            r0 = s_len - (i + 1 - c) * TQ
            g = g_ref[r0:r0 + TQ, :]
            s = s + jnp.concatenate([g] * HEADS_PER_BLOCK, axis=1)
        p = jnp.exp2(s)
        part = jnp.sum(p.reshape(TQ // SUBLANES, SUBLANES, p.shape[1]), axis=0)
        col_sum[n] = part if col_sum[n] is None else col_sum[n] + part
        p_scr[n % len(p_scr)][c * TQ:(c + 1) * TQ, :] = p.astype(BF16)

    def unit_steps(n):
        pr, i = units[n]
        qb = qq_ref[i * TQ:(i + 1) * TQ, pr * kw:(pr + 1) * kw]
        qcat = jnp.concatenate([jnp.where(q_masks[j], qb, jnp.zeros((), BF16))
                                for j in range(HEADS_PER_BLOCK)], axis=0)
        if qt_scr:
            qcat_t = qt_scr[n % len(qt_scr)]
            qcat_t[...] = qcat.T
            qcat = qcat_t
        return [functools.partial(chunk, n, c, qcat) for c in range(i + 1)]

    def finish(n):
        pr, i = units[n]
        q1 = (i + 1) * TQ
        o = jnp.dot(vt_scr[pr, :, 0:q1], p_scr[n % len(p_scr)][0:q1, :],
                    preferred_element_type=F32)
        sums = jnp.sum(col_sum[n], axis=0, keepdims=True)
        col_sum[n] = None
        res = []
        for j in range(HEADS_PER_BLOCK):
            num = o[j * HEAD_DIM:(j + 1) * HEAD_DIM, j * TQ:(j + 1) * TQ]
            res.append(num / sums[:, j * TQ:(j + 1) * TQ])
        o_ref[i * TQ:(i + 1) * TQ, pr * LANES:(pr + 1) * LANES] = (
            jnp.concatenate(res, axis=0).T.astype(o_ref.dtype))

    for n in range(len(units)):
        steps = unit_steps(n)
        pv_at = min(PV_LAG, len(steps) - 1)
        for k, step in enumerate(steps):
            step()
            if n > 0 and k == pv_at:
                finish(n - 1)
    finish(len(units) - 1)


def _staggered(stage_generators):
    waiting, active, done = list(stage_generators), [], object()
    while waiting or active:
        if waiting:
            active.append(waiting.pop(0))
        active = [g for g in active if next(g, done) is not done]


def _ffn_kernel(x_ref, of_ref, od_ref, gof_ref, god_ref, wof_ref, wod_ref, gffn_ref,
                wg_ref, wu_ref, wd_ref, out_ref):
    def norm(v, g):
        return v * lax.rsqrt(jnp.mean(v * v, axis=-1, keepdims=True) + EPS) * g

    rows = x_ref.shape[0] // FFN_SPLIT

    def row_group(rs):
        nf = norm(of_ref[rs, :].astype(F32), gof_ref[...]).astype(BF16)
        nd = norm(od_ref[rs, :].astype(F32), god_ref[...]).astype(BF16)
        x1 = (x_ref[rs, :] + jnp.dot(nf, wof_ref[...], preferred_element_type=F32)
              + jnp.dot(nd, wod_ref[...], preferred_element_type=F32))
        yield
        h = norm(x1, gffn_ref[...]).astype(BF16)
        a = jnp.dot(h, wg_ref[...], preferred_element_type=F32)
        u = jnp.dot(h, wu_ref[...], preferred_element_type=F32)
        yield
        g = (a * (1.0 / (1.0 + jnp.exp(-a))) * u).astype(BF16)
        out_ref[rs, :] = x1 + jnp.dot(g, wd_ref[...], preferred_element_type=F32)

    _staggered(row_group(slice(r * rows, (r + 1) * rows)) for r in range(FFN_SPLIT))


def _rope_tables(s_len):
    half = ROPE_DIM // 2
    inv_freq = jnp.power(jnp.float32(ROPE_THETA),
                         -jnp.arange(half, dtype=jnp.float32) * 2.0 / ROPE_DIM)
    ang = jnp.arange(s_len).astype(jnp.float32)[:, None] * inv_freq[None, :]
    cos, sin = jnp.cos(ang), jnp.sin(ang)
    ones = jnp.ones((s_len, HEAD_DIM - ROPE_DIM), F32)
    zeros = jnp.zeros((s_len, HEAD_DIM - ROPE_DIM), F32)
    zh = jnp.zeros((s_len, half), F32)
    c_head = jnp.concatenate([cos, cos, ones], axis=1)
    s1_head = jnp.concatenate([-sin, zh, zeros], axis=1)
    s2_head = jnp.concatenate([zh, sin, zeros], axis=1)
    rep = lambda a: jnp.tile(a, (1, HEADS_PER_BLOCK))
    return rep(c_head), rep(s1_head), rep(s2_head)


def _bias_tables(s_len):
    i = np.arange(TQ)[None, :]
    jp = np.arange(s_len)[:, None]
    delta = i - jp + s_len - TQ
    causal = np.where(delta >= 0, 0.0, NEG).astype(np.float32)
    mult = np.zeros(delta.shape, np.float64)
    for window, dil in DILATION_PAIRS:
        mult += (delta >= 0) & (delta <= window) & (delta % dil == 0)
    dil_bias = np.where(mult > 0, np.log2(np.maximum(mult, 1.0)), NEG).astype(np.float32)
    return causal, dil_bias, float(np.log2(len(DILATION_PAIRS)))


def _resident(shape):
    nd = len(shape)
    return pl.BlockSpec(shape, lambda *_: (0,) * nd, pipeline_mode=pl.Buffered(1))


def _layer(x, g_mix, w_in, b_forget, g_q_fox, g_k_fox, g_q_dil, g_k_dil,
           g_out_fox, g_out_dil, w_out, g_ffn, w_gate, w_up, w_down):
    b_sz, s_len, d = x.shape
    n_heads = b_forget.shape[0]
    w = n_heads * HEAD_DIM
    d_ff = w_gate.shape[1]
    assert w_in.shape[1] == 6 * w + n_heads and w % MXU_DIM == 0
    assert s_len % TM_PROJ == 0 and s_len % TQ == 0 and (b_sz * s_len) % TM_FFN == 0
    assert d % LANES == 0 and n_heads <= AUG_STRIDE

    cols = np.cumsum([0, w, w, w, n_heads, w, w, w])
    w_fox = w_in[:, cols[0]:cols[3]].astype(BF16)
    w_dil = w_in[:, cols[4]:cols[7]].astype(BF16)
    w_fa = jnp.zeros((d, LANES), F32).at[:, :n_heads].set(w_in[:, cols[3]:cols[4]]).astype(BF16)
    b_pad = jnp.zeros((1, LANES), F32).at[0, :n_heads].set(b_forget)
    scale = HEAD_DIM ** -0.5 * LOG2E
    gains = jnp.stack([jnp.tile(g_q_fox * scale, n_heads), jnp.tile(g_k_fox, n_heads),
                       jnp.tile(g_q_dil * scale, n_heads), jnp.tile(g_k_dil, n_heads)])
    head_id = np.arange(MXU_DIM) // HEAD_DIM
    bd = jnp.asarray((head_id[:, None] == head_id[None, :]) / HEAD_DIM, BF16)
    cos_t, s1_t, s2_t = _rope_tables(s_len)
    causal, dil_bias, dil_bias_max = _bias_tables(s_len)

    def logit_bound(gq, gk):
        return (HEAD_DIM * scale * BOUND_SLACK) * jnp.max(jnp.abs(gq)) * jnp.max(jnp.abs(gk))
    bound_fox = logit_bound(g_q_fox, g_k_fox)
    bound_dil = logit_bound(g_q_dil, g_k_dil)
    bounded_ok = 2.0 * jnp.maximum(bound_fox, bound_dil) + dil_bias_max <= EXP2_RANGE
    bnd_pad = jnp.zeros((1, LANES), F32).at[0, :n_heads].set(bound_fox)

    tok = lambda width: pl.BlockSpec((None, TM_PROJ, width), lambda b, t: (b, t, 0))
    pos = pl.BlockSpec((TM_PROJ, LANES), lambda b, t: (t, 0))
    bf = lambda width: jax.ShapeDtypeStruct((b_sz, s_len, width), BF16)
    qqf, kkf, vf, qd, kd, vd = pl.pallas_call(
        functools.partial(_proj_kernel, n_heads=n_heads),
        grid=(b_sz, s_len // TM_PROJ),
        in_specs=[tok(d), _resident((1, d)), _resident((d, 3 * w)), _resident((d, 3 * w)),
                  _resident((d, LANES)),
                  _resident((1, LANES)), _resident((1, LANES)), _resident((4, w)),
                  _resident((MXU_DIM, MXU_DIM)), pos, pos, pos],
        out_specs=[tok(2 * w), tok(2 * w), tok(w), tok(w), tok(w), tok(w)],
        out_shape=[bf(2 * w), bf(2 * w), bf(w), bf(w), bf(w), bf(w)],
        scratch_shapes=[pltpu.VMEM((8, LANES), F32)],
        compiler_params=pltpu.CompilerParams(
            dimension_semantics=("arbitrary", "arbitrary"), vmem_limit_bytes=VMEM_LIMIT),
        name="proj",
    )(x, g_mix.reshape(1, d), w_fox, w_dil, w_fa, b_pad, bnd_pad, gains, bd, cos_t, s1_t, s2_t)

    def attention(body, scratch, pairs, qq, kk, v, bias, dense_bias, name):
        kw = qq.shape[-1] // (w // LANES) * pairs
        blk = lambda width: pl.BlockSpec((None, s_len, width), lambda b, p: (b, 0, p))
        return pl.pallas_call(
            functools.partial(body, dense_bias=dense_bias),
            grid=(b_sz, w // LANES // pairs),
            in_specs=[blk(kw), blk(kw), blk(pairs * LANES), _resident((s_len, TQ))],
            out_specs=blk(pairs * LANES),
            out_shape=bf(w),
            scratch_shapes=scratch,
            compiler_params=pltpu.CompilerParams(
                dimension_semantics=("arbitrary", "arbitrary"), vmem_limit_bytes=VMEM_LIMIT),
            name=name,
        )(qq, kk, v, bias)

    nq = HEADS_PER_BLOCK * TQ
    vt_rows = LANES + ONES_ROWS
    exact_scratch = [pltpu.VMEM((vt_rows, s_len), BF16)] + [pltpu.VMEM((s_len, nq), F32)] * 3 + [
        pltpu.VMEM((vt_rows, nq), F32)]
    bounded_scratch = [pltpu.VMEM((PAIRS_PER_STEP, LANES, s_len), BF16)] + [
        pltpu.VMEM((s_len, nq), BF16)] * 2

    fox_scratch = bounded_scratch + [pltpu.VMEM((2 * LANES, nq), BF16)] * 2

    def attend_bounded():
        return (attention(_attn_bounded_kernel, fox_scratch, PAIRS_PER_STEP, qqf, kkf, vf,
                          jnp.asarray(causal), False, "attn_fox"),
                attention(_attn_bounded_kernel, bounded_scratch, PAIRS_PER_STEP, qd, kd, vd,
                          jnp.asarray(dil_bias) - (bound_dil + dil_bias_max), True, "attn_dil"))

    def attend_exact():
        return (attention(_attn_exact_kernel, exact_scratch, 1, qqf, kkf, vf,
                          jnp.asarray(causal), False, "attn_fox_exact"),
                attention(_attn_exact_kernel, exact_scratch, 1, qd, kd, vd,
                          jnp.asarray(dil_bias), True, "attn_dil_exact"))

    o_fox, o_dil = lax.cond(bounded_ok, attend_bounded, attend_exact)

    n_tok = b_sz * s_len
    rows = lambda width: pl.BlockSpec((TM_FFN, width), lambda t: (t, 0))
    out = pl.pallas_call(
        _ffn_kernel,
        grid=(n_tok // TM_FFN,),
        in_specs=[rows(d), rows(w), rows(w), _resident((1, w)), _resident((1, w)),
                  _resident((w, d)), _resident((w, d)), _resident((1, d)),
                  _resident((d, d_ff)), _resident((d, d_ff)), _resident((d_ff, d))],
        out_specs=rows(d),
        out_shape=jax.ShapeDtypeStruct((n_tok, d), x.dtype),
        compiler_params=pltpu.CompilerParams(
            dimension_semantics=("arbitrary",), vmem_limit_bytes=VMEM_LIMIT),
        name="ffn",
    )(x.reshape(n_tok, d), o_fox.reshape(n_tok, w), o_dil.reshape(n_tok, w),
      g_out_fox.reshape(1, w), g_out_dil.reshape(1, w),
      w_out[:w].astype(BF16), w_out[w:].astype(BF16), g_ffn.reshape(1, d),
      w_gate.astype(BF16), w_up.astype(BF16), w_down.astype(BF16))
    return out.reshape(b_sz, s_len, d)


def kernel(x, g_mix, w_in, b_forget, g_q_fox, g_k_fox, g_q_dil, g_k_dil, g_out_fox, g_out_dil,
           w_out, g_ffn, w_gate, w_up, w_down):
    for l in range(g_mix.shape[0]):
        x = _layer(x, g_mix[l], w_in[l], b_forget[l], g_q_fox[l], g_k_fox[l], g_q_dil[l],
                   g_k_dil[l], g_out_fox[l], g_out_dil[l], w_out[l], g_ffn[l], w_gate[l],
                   w_up[l], w_down[l])
    return x
```

```python
import functools
import math

import jax
import jax.numpy as jnp
import numpy as np
from jax import lax
from jax.experimental import pallas as pl
from jax.experimental.pallas import tpu as pltpu

F32 = jnp.float32
BF16 = jnp.bfloat16

HEAD_DIM = 64
ROPE_DIM = HEAD_DIM // 4
ROPE_THETA = 500000.0
DILATION_PAIRS = ((128, 1), (512, 4), (2048, 16))
EPS = 1e-6
NEG = -1e30
LOG2E = math.log2(math.e)

LANES = 128
SUBLANES = 8
MXU_DIM = 256
HEADS_PER_BLOCK = LANES // HEAD_DIM
ONES_ROWS = 16
AUG_TERMS = 3
AUG_STRIDE = 8
Q_ONES = 4 * AUG_STRIDE

TM_PROJ = 1024
PROJ_SPLIT = 2
TQ = 256
TM_FFN = 1024
FFN_SPLIT = 4
VMEM_LIMIT = 56 * 1024 * 1024
PV_LAG = 1
PAIRS_PER_STEP = 2
PAIRS_PER_STEP_DIL = 4
EXP2_RANGE = 64.0
BOUND_SLACK = 1.02


def _split3(v):
    hi = v.astype(BF16).astype(F32)
    r1 = v - hi
    mid = r1.astype(BF16).astype(F32)
    lo = (r1 - mid).astype(BF16).astype(F32)
    return hi, mid, lo


def _pack3(v):
    hi, mid, lo = _split3(v)
    return hi + pltpu.roll(mid, AUG_STRIDE, 1) + pltpu.roll(lo, 2 * AUG_STRIDE, 1)


def _round_robin(stage_generators):
    active, done = list(stage_generators), object()
    while active:
        active = [g for g in active if next(g, done) is not done]


def _proj_kernel(x_ref, gmix_ref, wfox_ref, wdil_ref, wfa_ref, bf_ref, bnd_ref, gains_ref, bd_ref,
                 cos_ref, s1_ref, s2_ref,
                 qqf_ref, kkf_ref, vf_ref, qd_ref, kd_ref, vd_ref, carry_ref, *, n_heads):
    t = pl.program_id(1)
    tm, d = x_ref.shape
    rows = tm // PROJ_SPLIT
    w = n_heads * HEAD_DIM
    n_blk = w // LANES
    lane = lax.broadcasted_iota(jnp.int32, (rows, LANES), 1)
    row = lax.broadcasted_iota(jnp.int32, (rows, LANES), 0)
    term_lane = (lane % AUG_STRIDE < HEADS_PER_BLOCK) & (lane < AUG_TERMS * AUG_STRIDE)
    ones_lane = ((lane % AUG_STRIDE < HEADS_PER_BLOCK) & (lane >= Q_ONES)
                 & (lane < Q_ONES + AUG_TERMS * AUG_STRIDE))

    @pl.when(t == 0)
    def _():
        carry_ref[...] = jnp.zeros_like(carry_ref)

    def mean_squares(p):
        sq = (p * p).astype(BF16)
        parts = [jnp.dot(sq[:, j * MXU_DIM:(j + 1) * MXU_DIM], bd_ref[...],
                         preferred_element_type=F32) for j in range(w // MXU_DIM)]
        return jnp.concatenate(parts, axis=1)

    def rope(xc, rs):
        return (xc * cos_ref[rs, :] + pltpu.roll(xc, LANES - ROPE_DIM // 2, 1) * s1_ref[rs, :]
                + pltpu.roll(xc, ROPE_DIM // 2, 1) * s2_ref[rs, :])

    def row_group(rs):
        x = x_ref[rs, :]
        ms = jnp.mean(x * x, axis=-1, keepdims=True)
        h = (x * lax.rsqrt(ms + EPS) * gmix_ref[...]).astype(BF16)

        def group(i):
            w_ref, k = (wfox_ref, i) if i < 3 else (wdil_ref, i - 3)
            return jnp.dot(h, w_ref[:, k * w:(k + 1) * w], preferred_element_type=F32)

        half = d // 2
        fa = (jnp.dot(h[:, :half], wfa_ref[:half, :], preferred_element_type=F32)
              + jnp.dot(h[:, half:], wfa_ref[half:, :], preferred_element_type=F32)
              + bf_ref[...])
        g_q = group(0)
        yield

        logf = jnp.minimum(fa, 0.0) - jnp.log1p(jnp.exp(-jnp.abs(fa)))
        logf = jnp.where(lane < n_heads, logf * LOG2E, 0.0)
        g_k = group(1)
        yield

        ctile = logf
        shift = 1
        while shift < rows:
            ctile = ctile + jnp.where(row >= shift, pltpu.roll(ctile, shift, 0), 0.0)
            shift *= 2
        c = jnp.where(lane < n_heads, ctile + carry_ref[0:1, :], 0.0)
        carry_ref[...] = jnp.broadcast_to(c[rows - 1:rows, :], carry_ref.shape)
        cq = _pack3(c)
        ck = _pack3(jnp.where(lane < n_heads, c + bnd_ref[...], 0.0))

        def gate_lanes(p):
            shift = HEADS_PER_BLOCK * p
            qa = jnp.where(term_lane, pltpu.roll(cq, (LANES - shift) % LANES, 1),
                           jnp.where(ones_lane, 1.0, 0.0))
            ka = jnp.where(ones_lane, -pltpu.roll(ck, (Q_ONES - shift) % LANES, 1),
                           jnp.where(term_lane, 1.0, 0.0))
            return qa.astype(BF16), ka.astype(BF16)

        ms_q = mean_squares(g_q)
        g_v = group(2)
        yield
        qf = g_q * lax.rsqrt(ms_q + EPS) * gains_ref[0:1, :]
        ms_k = mean_squares(g_k)
        vf_ref[rs, :] = g_v.astype(BF16)
        g_qd = group(3)
        yield
        kf = g_k * lax.rsqrt(ms_k + EPS) * gains_ref[1:2, :]
        for j in range(n_blk):
            qa, ka = gate_lanes(j)
            qqf_ref[rs, 2 * j * LANES:(2 * j + 1) * LANES] = qf[:, j * LANES:(j + 1) * LANES].astype(BF16)
            qqf_ref[rs, (2 * j + 1) * LANES:(2 * j + 2) * LANES] = qa
            kkf_ref[rs, 2 * j * LANES:(2 * j + 1) * LANES] = kf[:, j * LANES:(j + 1) * LANES].astype(BF16)
            kkf_ref[rs, (2 * j + 1) * LANES:(2 * j + 2) * LANES] = ka

        ms_qd = mean_squares(g_qd)
        g_kd = group(4)
        yield
        qd = g_qd * lax.rsqrt(ms_qd + EPS) * gains_ref[2:3, :]
        ms_kd = mean_squares(g_kd)
        g_vd = group(5)
        yield
        kd = g_kd * lax.rsqrt(ms_kd + EPS) * gains_ref[3:4, :]
        for j in range(n_blk):
            qd_ref[rs, j * LANES:(j + 1) * LANES] = rope(qd[:, j * LANES:(j + 1) * LANES], rs).astype(BF16)
            kd_ref[rs, j * LANES:(j + 1) * LANES] = rope(kd[:, j * LANES:(j + 1) * LANES], rs).astype(BF16)
        vd_ref[rs, :] = g_vd.astype(BF16)

    _round_robin(row_group(slice(r * rows, (r + 1) * rows)) for r in range(PROJ_SPLIT))


def _attn_exact_kernel(qq_ref, kk_ref, v_ref, g_ref, o_ref, vt_scr, s_scr0, s_scr1, s_scr2,
                       acc_scr, *, dense_bias):
    s_len, kw = qq_ref.shape
    n_units = s_len // TQ
    lane_q = lax.broadcasted_iota(jnp.int32, (TQ, kw), 1)
    vt_scr[0:LANES, :] = v_ref[...].T
    vt_scr[LANES:, :] = jnp.ones((ONES_ROWS, s_len), BF16)

    q_masks = []
    for j in range(HEADS_PER_BLOCK):
        m = (lane_q >= j * HEAD_DIM) & (lane_q < (j + 1) * HEAD_DIM)
        if kw > LANES:
            m = m | ((lane_q >= LANES) & (lane_q % AUG_STRIDE == j))
        q_masks.append(m)

    s_scr = (s_scr0, s_scr1, s_scr2)
    col_max = [None] * n_units

    def scores(i, c, qcat):
        s = lax.dot_general(kk_ref[c * TQ:(c + 1) * TQ, :], qcat, (((1,), (1,)), ((), ())),
                            preferred_element_type=F32)
        if dense_bias or c == i:
            r0 = s_len - (i + 1 - c) * TQ
            g = g_ref[r0:r0 + TQ, :]
            s = s + jnp.concatenate([g] * HEADS_PER_BLOCK, axis=1)
        s_scr[i % len(s_scr)][c * TQ:(c + 1) * TQ, :] = s
        cm = jnp.max(s, axis=0, keepdims=True)
        col_max[i] = cm if col_max[i] is None else jnp.maximum(col_max[i], cm)

    def probs(i, c):
        return jnp.exp2((s_scr[i % len(s_scr)][c * TQ:(c + 1) * TQ, :] - col_max[i]).astype(BF16))

    def weighted_values(c, p, first):
        o = jnp.dot(vt_scr[:, c * TQ:(c + 1) * TQ], p, preferred_element_type=F32)
        acc_scr[...] = o if first else acc_scr[...] + o

    def score_steps(i):
        qb = qq_ref[i * TQ:(i + 1) * TQ, :]
        qcat = jnp.concatenate([jnp.where(q_masks[j], qb, jnp.zeros((), BF16))
                                for j in range(HEADS_PER_BLOCK)], axis=0)
        return [functools.partial(scores, i, c, qcat) for c in range(i + 1)]

    def finish(i):
        res = []
        for j in range(HEADS_PER_BLOCK):
            num = acc_scr[j * HEAD_DIM:(j + 1) * HEAD_DIM, j * TQ:(j + 1) * TQ]
            den = acc_scr[LANES:LANES + 1, j * TQ:(j + 1) * TQ]
            res.append(num / den)
        o_ref[i * TQ:(i + 1) * TQ, :] = jnp.concatenate(res, axis=0).T.astype(o_ref.dtype)
        col_max[i] = None

    lead = len(s_scr) - 1
    for u in range(min(lead, n_units)):
        for step in score_steps(u):
            step()
    for i in range(n_units):
        n_chunks = i + 1
        nxt = score_steps(i + lead) if i + lead < n_units else []
        p_prev = None
        for k in range(max(n_chunks + 1, len(nxt))):
            if k < len(nxt):
                nxt[k]()
            if p_prev is not None:
                weighted_values(k - 1, p_prev, k == 1)
            p_prev = probs(i, k) if k < n_chunks else None
        finish(i)


def _attn_bounded_kernel(qq_ref, kk_ref, v_ref, g_ref, o_ref, vt_scr, p_scr0, p_scr1,
                         *qt_scr, dense_bias):
    s_len = qq_ref.shape[0]
    n_pairs = v_ref.shape[1] // LANES
    kw = qq_ref.shape[1] // n_pairs
    n_blocks = s_len // TQ
    lane_q = lax.broadcasted_iota(jnp.int32, (TQ, kw), 1)
    for pr in range(n_pairs):
        vt_scr[pr] = v_ref[:, pr * LANES:(pr + 1) * LANES].T

    q_masks = []
    for j in range(HEADS_PER_BLOCK):
        m = (lane_q >= j * HEAD_DIM) & (lane_q < (j + 1) * HEAD_DIM)
        if kw > LANES:
            m = m | ((lane_q >= LANES) & (lane_q % AUG_STRIDE == j))
        q_masks.append(m)
    p_scr = (p_scr0, p_scr1)
    units = [(pr, i) for pr in range(n_pairs) for i in range(n_blocks)]
    col_sum = [None] * len(units)

    def chunk(n, c, qcat):
        pr, i = units[n]
        keys = kk_ref[c * TQ:(c + 1) * TQ, pr * kw:(pr + 1) * kw]
        if qt_scr:
            s = jnp.dot(keys, qcat[...], preferred_element_type=F32)
        else:
            s = lax.dot_general(keys, qcat, (((1,), (1,)), ((), ())),
                                preferred_element_type=F32)
        if dense_bias or c == i:
            r0 = s_len - (i + 1 - c) * TQ
            g = g_ref[r0:r0 + TQ, :]
            s = s + jnp.concatenate([g] * HEADS_PER_BLOCK, axis=1)
        p = jnp.exp2(s)
        part = jnp.sum(p.reshape(TQ // SUBLANES, SUBLANES, p.shape[1]), axis=0)
        col_sum[n] = part if col_sum[n] is None else col_sum[n] + part
        p_scr[n % len(p_scr)][c * TQ:(c + 1) * TQ, :] = p.astype(BF16)

    def unit_steps(n):
        pr, i = units[n]
        qb = qq_ref[i * TQ:(i + 1) * TQ, pr * kw:(pr + 1) * kw]
        qcat = jnp.concatenate([jnp.where(q_masks[j], qb, jnp.zeros((), BF16))
                                for j in range(HEADS_PER_BLOCK)], axis=0)
        if qt_scr:
            qcat_t = qt_scr[n % len(qt_scr)]
            qcat_t[...] = qcat.T
            qcat = qcat_t
        return [functools.partial(chunk, n, c, qcat) for c in range(i + 1)]

    def finish(n):
        pr, i = units[n]
        q1 = (i + 1) * TQ
        o = jnp.dot(vt_scr[pr, :, 0:q1], p_scr[n % len(p_scr)][0:q1, :],
                    preferred_element_type=F32)
        sums = jnp.sum(col_sum[n], axis=0, keepdims=True)
        col_sum[n] = None
        res = []
        for j in range(HEADS_PER_BLOCK):
            num = o[j * HEAD_DIM:(j + 1) * HEAD_DIM, j * TQ:(j + 1) * TQ]
            res.append(num / sums[:, j * TQ:(j + 1) * TQ])
        o_ref[i * TQ:(i + 1) * TQ, pr * LANES:(pr + 1) * LANES] = (
            jnp.concatenate(res, axis=0).T.astype(o_ref.dtype))

    for n in range(len(units)):
        steps = unit_steps(n)
        pv_at = min(PV_LAG, len(steps) - 1)
        for k, step in enumerate(steps):
            step()
            if n > 0 and k == pv_at:
                finish(n - 1)
    finish(len(units) - 1)


def _staggered(stage_generators):
    waiting, active, done = list(stage_generators), [], object()
    while waiting or active:
        if waiting:
            active.append(waiting.pop(0))
        active = [g for g in active if next(g, done) is not done]


def _ffn_kernel(x_ref, of_ref, od_ref, gof_ref, god_ref, wof_ref, wod_ref, gffn_ref,
                wg_ref, wu_ref, wd_ref, out_ref):
    def norm(v, g):
        return v * lax.rsqrt(jnp.mean(v * v, axis=-1, keepdims=True) + EPS) * g

    rows = x_ref.shape[0] // FFN_SPLIT

    def row_group(rs):
        nf = norm(of_ref[rs, :].astype(F32), gof_ref[...]).astype(BF16)
        nd = norm(od_ref[rs, :].astype(F32), god_ref[...]).astype(BF16)
        x1 = (x_ref[rs, :] + jnp.dot(nf, wof_ref[...], preferred_element_type=F32)
              + jnp.dot(nd, wod_ref[...], preferred_element_type=F32))
        yield
        h = norm(x1, gffn_ref[...]).astype(BF16)
        a = jnp.dot(h, wg_ref[...], preferred_element_type=F32)
        u = jnp.dot(h, wu_ref[...], preferred_element_type=F32)
        yield
        g = (a * (1.0 / (1.0 + jnp.exp(-a))) * u).astype(BF16)
        out_ref[rs, :] = x1 + jnp.dot(g, wd_ref[...], preferred_element_type=F32)

    _staggered(row_group(slice(r * rows, (r + 1) * rows)) for r in range(FFN_SPLIT))


def _rope_tables(s_len):
    half = ROPE_DIM // 2
    inv_freq = jnp.power(jnp.float32(ROPE_THETA),
                         -jnp.arange(half, dtype=jnp.float32) * 2.0 / ROPE_DIM)
    ang = jnp.arange(s_len).astype(jnp.float32)[:, None] * inv_freq[None, :]
    cos, sin = jnp.cos(ang), jnp.sin(ang)
    ones = jnp.ones((s_len, HEAD_DIM - ROPE_DIM), F32)
    zeros = jnp.zeros((s_len, HEAD_DIM - ROPE_DIM), F32)
    zh = jnp.zeros((s_len, half), F32)
    c_head = jnp.concatenate([cos, cos, ones], axis=1)
    s1_head = jnp.concatenate([-sin, zh, zeros], axis=1)
    s2_head = jnp.concatenate([zh, sin, zeros], axis=1)
    rep = lambda a: jnp.tile(a, (1, HEADS_PER_BLOCK))
    return rep(c_head), rep(s1_head), rep(s2_head)


def _bias_tables(s_len):
    i = np.arange(TQ)[None, :]
    jp = np.arange(s_len)[:, None]
    delta = i - jp + s_len - TQ
    causal = np.where(delta >= 0, 0.0, NEG).astype(np.float32)
    mult = np.zeros(delta.shape, np.float64)
    for window, dil in DILATION_PAIRS:
        mult += (delta >= 0) & (delta <= window) & (delta % dil == 0)
    dil_bias = np.where(mult > 0, np.log2(np.maximum(mult, 1.0)), NEG).astype(np.float32)
    return causal, dil_bias, float(np.log2(len(DILATION_PAIRS)))


def _resident(shape):
    nd = len(shape)
    return pl.BlockSpec(shape, lambda *_: (0,) * nd, pipeline_mode=pl.Buffered(1))


def _layer(x, g_mix, w_in, b_forget, g_q_fox, g_k_fox, g_q_dil, g_k_dil,
           g_out_fox, g_out_dil, w_out, g_ffn, w_gate, w_up, w_down):
    b_sz, s_len, d = x.shape
    n_heads = b_forget.shape[0]
    w = n_heads * HEAD_DIM
    d_ff = w_gate.shape[1]
    assert w_in.shape[1] == 6 * w + n_heads and w % MXU_DIM == 0
    assert s_len % TM_PROJ == 0 and s_len % TQ == 0 and (b_sz * s_len) % TM_FFN == 0
    assert d % LANES == 0 and n_heads <= AUG_STRIDE

    cols = np.cumsum([0, w, w, w, n_heads, w, w, w])
    w_fox = w_in[:, cols[0]:cols[3]].astype(BF16)
    w_dil = w_in[:, cols[4]:cols[7]].astype(BF16)
    w_fa = jnp.zeros((d, LANES), F32).at[:, :n_heads].set(w_in[:, cols[3]:cols[4]]).astype(BF16)
    b_pad = jnp.zeros((1, LANES), F32).at[0, :n_heads].set(b_forget)
    scale = HEAD_DIM ** -0.5 * LOG2E
    gains = jnp.stack([jnp.tile(g_q_fox * scale, n_heads), jnp.tile(g_k_fox, n_heads),
                       jnp.tile(g_q_dil * scale, n_heads), jnp.tile(g_k_dil, n_heads)])
    head_id = np.arange(MXU_DIM) // HEAD_DIM
    bd = jnp.asarray((head_id[:, None] == head_id[None, :]) / HEAD_DIM, BF16)
    cos_t, s1_t, s2_t = _rope_tables(s_len)
    causal, dil_bias, dil_bias_max = _bias_tables(s_len)

    def logit_bound(gq, gk):
        return (HEAD_DIM * scale * BOUND_SLACK) * jnp.max(jnp.abs(gq)) * jnp.max(jnp.abs(gk))
    bound_fox = logit_bound(g_q_fox, g_k_fox)
    bound_dil = logit_bound(g_q_dil, g_k_dil)
    bounded_ok = 2.0 * jnp.maximum(bound_fox, bound_dil) + dil_bias_max <= EXP2_RANGE
    bnd_pad = jnp.zeros((1, LANES), F32).at[0, :n_heads].set(bound_fox)

    tok = lambda width: pl.BlockSpec((None, TM_PROJ, width), lambda b, t: (b, t, 0))
    pos = pl.BlockSpec((TM_PROJ, LANES), lambda b, t: (t, 0))
    bf = lambda width: jax.ShapeDtypeStruct((b_sz, s_len, width), BF16)
    qqf, kkf, vf, qd, kd, vd = pl.pallas_call(
        functools.partial(_proj_kernel, n_heads=n_heads),
        grid=(b_sz, s_len // TM_PROJ),
        in_specs=[tok(d), _resident((1, d)), _resident((d, 3 * w)), _resident((d, 3 * w)),
                  _resident((d, LANES)),
                  _resident((1, LANES)), _resident((1, LANES)), _resident((4, w)),
                  _resident((MXU_DIM, MXU_DIM)), pos, pos, pos],
        out_specs=[tok(2 * w), tok(2 * w), tok(w), tok(w), tok(w), tok(w)],
        out_shape=[bf(2 * w), bf(2 * w), bf(w), bf(w), bf(w), bf(w)],
        scratch_shapes=[pltpu.VMEM((8, LANES), F32)],
        compiler_params=pltpu.CompilerParams(
            dimension_semantics=("arbitrary", "arbitrary"), vmem_limit_bytes=VMEM_LIMIT),
        name="proj",
    )(x, g_mix.reshape(1, d), w_fox, w_dil, w_fa, b_pad, bnd_pad, gains, bd, cos_t, s1_t, s2_t)

    def attention(body, scratch, pairs, qq, kk, v, bias, dense_bias, name):
        kw = qq.shape[-1] // (w // LANES) * pairs
        blk = lambda width: pl.BlockSpec((None, s_len, width), lambda b, p: (b, 0, p))
        return pl.pallas_call(
            functools.partial(body, dense_bias=dense_bias),
            grid=(b_sz, w // LANES // pairs),
            in_specs=[blk(kw), blk(kw), blk(pairs * LANES), _resident((s_len, TQ))],
            out_specs=blk(pairs * LANES),
            out_shape=bf(w),
            scratch_shapes=scratch,
            compiler_params=pltpu.CompilerParams(
                dimension_semantics=("arbitrary", "arbitrary"), vmem_limit_bytes=VMEM_LIMIT),
            name=name,
        )(qq, kk, v, bias)

    nq = HEADS_PER_BLOCK * TQ
    vt_rows = LANES + ONES_ROWS
    exact_scratch = [pltpu.VMEM((vt_rows, s_len), BF16)] + [pltpu.VMEM((s_len, nq), F32)] * 3 + [
        pltpu.VMEM((vt_rows, nq), F32)]
    def bounded_scratch(pairs):
        return [pltpu.VMEM((pairs, LANES, s_len), BF16)] + [pltpu.VMEM((s_len, nq), BF16)] * 2

    fox_scratch = bounded_scratch(PAIRS_PER_STEP) + [pltpu.VMEM((2 * LANES, nq), BF16)] * 2

    def attend_bounded():
        return (attention(_attn_bounded_kernel, fox_scratch, PAIRS_PER_STEP, qqf, kkf, vf,
                          jnp.asarray(causal), False, "attn_fox"),
                attention(_attn_bounded_kernel, bounded_scratch(PAIRS_PER_STEP_DIL),
                          PAIRS_PER_STEP_DIL, qd, kd, vd,
                          jnp.asarray(dil_bias) - (bound_dil + dil_bias_max), True, "attn_dil"))

    def attend_exact():
        return (attention(_attn_exact_kernel, exact_scratch, 1, qqf, kkf, vf,
                          jnp.asarray(causal), False, "attn_fox_exact"),
                attention(_attn_exact_kernel, exact_scratch, 1, qd, kd, vd,
                          jnp.asarray(dil_bias), True, "attn_dil_exact"))

    o_fox, o_dil = lax.cond(bounded_ok, attend_bounded, attend_exact)

    n_tok = b_sz * s_len
    rows = lambda width: pl.BlockSpec((TM_FFN, width), lambda t: (t, 0))
    out = pl.pallas_call(
        _ffn_kernel,
        grid=(n_tok // TM_FFN,),
        in_specs=[rows(d), rows(w), rows(w), _resident((1, w)), _resident((1, w)),
                  _resident((w, d)), _resident((w, d)), _resident((1, d)),
                  _resident((d, d_ff)), _resident((d, d_ff)), _resident((d_ff, d))],
        out_specs=rows(d),
        out_shape=jax.ShapeDtypeStruct((n_tok, d), x.dtype),
        compiler_params=pltpu.CompilerParams(
            dimension_semantics=("arbitrary",), vmem_limit_bytes=VMEM_LIMIT),
        name="ffn",
    )(x.reshape(n_tok, d), o_fox.reshape(n_tok, w), o_dil.reshape(n_tok, w),
      g_out_fox.reshape(1, w), g_out_dil.reshape(1, w),
      w_out[:w].astype(BF16), w_out[w:].astype(BF16), g_ffn.reshape(1, d),
      w_gate.astype(BF16), w_up.astype(BF16), w_down.astype(BF16))
    return out.reshape(b_sz, s_len, d)


def kernel(x, g_mix, w_in, b_forget, g_q_fox, g_k_fox, g_q_dil, g_k_dil, g_out_fox, g_out_dil,
           w_out, g_ffn, w_gate, w_up, w_down):
    for l in range(g_mix.shape[0]):
        x = _layer(x, g_mix[l], w_in[l], b_forget[l], g_q_fox[l], g_k_fox[l], g_q_dil[l],
                   g_k_dil[l], g_out_fox[l], g_out_dil[l], w_out[l], g_ffn[l], w_gate[l],
                   w_up[l], w_down[l])
    return x
```

```python
import functools
import math

import jax
import jax.numpy as jnp
import numpy as np
from jax import lax
from jax.experimental import pallas as pl
from jax.experimental.pallas import tpu as pltpu

F32 = jnp.float32
BF16 = jnp.bfloat16

HEAD_DIM = 64
ROPE_DIM = HEAD_DIM // 4
ROPE_THETA = 500000.0
DILATION_PAIRS = ((128, 1), (512, 4), (2048, 16))
EPS = 1e-6
NEG = -1e30
LOG2E = math.log2(math.e)

LANES = 128
SUBLANES = 8
MXU_DIM = 256
HEADS_PER_BLOCK = LANES // HEAD_DIM
ONES_ROWS = 16
AUG_TERMS = 3
AUG_STRIDE = 8
Q_ONES = 4 * AUG_STRIDE

TM_PROJ = 1024
PROJ_SPLIT = 2
TQ = 256
TM_FFN = 1024
FFN_SPLIT = 4
VMEM_LIMIT = 56 * 1024 * 1024
PV_LAG = 1
PAIRS_PER_STEP = 2
EXP2_RANGE = 64.0
BOUND_SLACK = 1.02


def _split3(v):
    hi = v.astype(BF16).astype(F32)
    r1 = v - hi
    mid = r1.astype(BF16).astype(F32)
    lo = (r1 - mid).astype(BF16).astype(F32)
    return hi, mid, lo


def _pack3(v):
    hi, mid, lo = _split3(v)
    return hi + pltpu.roll(mid, AUG_STRIDE, 1) + pltpu.roll(lo, 2 * AUG_STRIDE, 1)


def _round_robin(stage_generators):
    active, done = list(stage_generators), object()
    while active:
        active = [g for g in active if next(g, done) is not done]


def _proj_kernel(x_ref, wfox_ref, wdil_ref, wfa_ref, bf_ref, bnd_ref, gains_ref, bd_ref,
                 cos_ref, s1_ref, s2_ref,
                 qqf_ref, kkf_ref, vf_ref, qd_ref, kd_ref, vd_ref, carry_ref, *, n_heads):
    t = pl.program_id(1)
    tm, d = x_ref.shape
    rows = tm // PROJ_SPLIT
    w = n_heads * HEAD_DIM
    n_blk = w // LANES
    lane = lax.broadcasted_iota(jnp.int32, (rows, LANES), 1)
    row = lax.broadcasted_iota(jnp.int32, (rows, LANES), 0)
    term_lane = (lane % AUG_STRIDE < HEADS_PER_BLOCK) & (lane < AUG_TERMS * AUG_STRIDE)
    ones_lane = ((lane % AUG_STRIDE < HEADS_PER_BLOCK) & (lane >= Q_ONES)
                 & (lane < Q_ONES + AUG_TERMS * AUG_STRIDE))

    @pl.when(t == 0)
    def _():
        carry_ref[...] = jnp.zeros_like(carry_ref)

    def mean_squares(p):
        sq = (p * p).astype(BF16)
        parts = [jnp.dot(sq[:, j * MXU_DIM:(j + 1) * MXU_DIM], bd_ref[...],
                         preferred_element_type=F32) for j in range(w // MXU_DIM)]
        return jnp.concatenate(parts, axis=1)

    def rope(xc, rs):
        return (xc * cos_ref[rs, :] + pltpu.roll(xc, LANES - ROPE_DIM // 2, 1) * s1_ref[rs, :]
                + pltpu.roll(xc, ROPE_DIM // 2, 1) * s2_ref[rs, :])

    def row_group(rs):
        x = x_ref[rs, :]
        ms = jnp.mean(x * x, axis=-1, keepdims=True)
        h = (x * lax.rsqrt(ms + EPS)).astype(BF16)

        def group(i):
            w_ref, k = (wfox_ref, i) if i < 3 else (wdil_ref, i - 3)
            return jnp.dot(h, w_ref[:, k * w:(k + 1) * w], preferred_element_type=F32)

        half = d // 2
        fa = (jnp.dot(h[:, :half], wfa_ref[:half, :], preferred_element_type=F32)
              + jnp.dot(h[:, half:], wfa_ref[half:, :], preferred_element_type=F32)
              + bf_ref[...])
        g_q = group(0)
        yield

        logf = jnp.minimum(fa, 0.0) - jnp.log1p(jnp.exp(-jnp.abs(fa)))
        logf = jnp.where(lane < n_heads, logf * LOG2E, 0.0)
        g_k = group(1)
        yield

        ctile = logf
        shift = 1
        while shift < rows:
            ctile = ctile + jnp.where(row >= shift, pltpu.roll(ctile, shift, 0), 0.0)
            shift *= 2
        c = jnp.where(lane < n_heads, ctile + carry_ref[0:1, :], 0.0)
        carry_ref[...] = jnp.broadcast_to(c[rows - 1:rows, :], carry_ref.shape)
        cq = _pack3(c)
        ck = _pack3(jnp.where(lane < n_heads, c + bnd_ref[...], 0.0))

        def gate_lanes(p):
            shift = HEADS_PER_BLOCK * p
            qa = jnp.where(term_lane, pltpu.roll(cq, (LANES - shift) % LANES, 1),
                           jnp.where(ones_lane, 1.0, 0.0))
            ka = jnp.where(ones_lane, -pltpu.roll(ck, (Q_ONES - shift) % LANES, 1),
                           jnp.where(term_lane, 1.0, 0.0))
            return qa.astype(BF16), ka.astype(BF16)

        ms_q = mean_squares(g_q)
        g_v = group(2)
        yield
        qf = g_q * lax.rsqrt(ms_q + EPS) * gains_ref[0:1, :]
        ms_k = mean_squares(g_k)
        vf_ref[rs, :] = g_v.astype(BF16)
        g_qd = group(3)
        yield
        kf = g_k * lax.rsqrt(ms_k + EPS) * gains_ref[1:2, :]
        for j in range(n_blk):
            qa, ka = gate_lanes(j)
            qqf_ref[rs, 2 * j * LANES:(2 * j + 1) * LANES] = qf[:, j * LANES:(j + 1) * LANES].astype(BF16)
            qqf_ref[rs, (2 * j + 1) * LANES:(2 * j + 2) * LANES] = qa
            kkf_ref[rs, 2 * j * LANES:(2 * j + 1) * LANES] = kf[:, j * LANES:(j + 1) * LANES].astype(BF16)
            kkf_ref[rs, (2 * j + 1) * LANES:(2 * j + 2) * LANES] = ka

        ms_qd = mean_squares(g_qd)
        g_kd = group(4)
        yield
        qd = g_qd * lax.rsqrt(ms_qd + EPS) * gains_ref[2:3, :]
        ms_kd = mean_squares(g_kd)
        g_vd = group(5)
        yield
        kd = g_kd * lax.rsqrt(ms_kd + EPS) * gains_ref[3:4, :]
        for j in range(n_blk):
            qd_ref[rs, j * LANES:(j + 1) * LANES] = rope(qd[:, j * LANES:(j + 1) * LANES], rs).astype(BF16)
            kd_ref[rs, j * LANES:(j + 1) * LANES] = rope(kd[:, j * LANES:(j + 1) * LANES], rs).astype(BF16)
        vd_ref[rs, :] = g_vd.astype(BF16)

    _round_robin(row_group(slice(r * rows, (r + 1) * rows)) for r in range(PROJ_SPLIT))


def _attn_exact_kernel(qq_ref, kk_ref, v_ref, g_ref, o_ref, vt_scr, s_scr0, s_scr1, s_scr2,
                       acc_scr, *, dense_bias):
    s_len, kw = qq_ref.shape
    n_units = s_len // TQ
    lane_q = lax.broadcasted_iota(jnp.int32, (TQ, kw), 1)
    vt_scr[0:LANES, :] = v_ref[...].T
    vt_scr[LANES:, :] = jnp.ones((ONES_ROWS, s_len), BF16)

    q_masks = []
    for j in range(HEADS_PER_BLOCK):
        m = (lane_q >= j * HEAD_DIM) & (lane_q < (j + 1) * HEAD_DIM)
        if kw > LANES:
            m = m | ((lane_q >= LANES) & (lane_q % AUG_STRIDE == j))
        q_masks.append(m)

    s_scr = (s_scr0, s_scr1, s_scr2)
    col_max = [None] * n_units

    def scores(i, c, qcat):
        s = lax.dot_general(kk_ref[c * TQ:(c + 1) * TQ, :], qcat, (((1,), (1,)), ((), ())),
                            preferred_element_type=F32)
        if dense_bias or c == i:
            r0 = s_len - (i + 1 - c) * TQ
            g = g_ref[r0:r0 + TQ, :]
            s = s + jnp.concatenate([g] * HEADS_PER_BLOCK, axis=1)
        s_scr[i % len(s_scr)][c * TQ:(c + 1) * TQ, :] = s
        cm = jnp.max(s, axis=0, keepdims=True)
        col_max[i] = cm if col_max[i] is None else jnp.maximum(col_max[i], cm)

    def probs(i, c):
        return jnp.exp2((s_scr[i % len(s_scr)][c * TQ:(c + 1) * TQ, :] - col_max[i]).astype(BF16))

    def weighted_values(c, p, first):
        o = jnp.dot(vt_scr[:, c * TQ:(c + 1) * TQ], p, preferred_element_type=F32)
        acc_scr[...] = o if first else acc_scr[...] + o

    def score_steps(i):
        qb = qq_ref[i * TQ:(i + 1) * TQ, :]
        qcat = jnp.concatenate([jnp.where(q_masks[j], qb, jnp.zeros((), BF16))
                                for j in range(HEADS_PER_BLOCK)], axis=0)
        return [functools.partial(scores, i, c, qcat) for c in range(i + 1)]

    def finish(i):
        res = []
        for j in range(HEADS_PER_BLOCK):
            num = acc_scr[j * HEAD_DIM:(j + 1) * HEAD_DIM, j * TQ:(j + 1) * TQ]
            den = acc_scr[LANES:LANES + 1, j * TQ:(j + 1) * TQ]
            res.append(num / den)
        o_ref[i * TQ:(i + 1) * TQ, :] = jnp.concatenate(res, axis=0).T.astype(o_ref.dtype)
        col_max[i] = None

    lead = len(s_scr) - 1
    for u in range(min(lead, n_units)):
        for step in score_steps(u):
            step()
    for i in range(n_units):
        n_chunks = i + 1
        nxt = score_steps(i + lead) if i + lead < n_units else []
        p_prev = None
        for k in range(max(n_chunks + 1, len(nxt))):
            if k < len(nxt):
                nxt[k]()
            if p_prev is not None:
                weighted_values(k - 1, p_prev, k == 1)
            p_prev = probs(i, k) if k < n_chunks else None
        finish(i)


def _attn_bounded_kernel(qq_ref, kk_ref, v_ref, g_ref, o_ref, vt_scr, p_scr0, p_scr1,
                         *qt_scr, dense_bias):
    s_len = qq_ref.shape[0]
    n_pairs = v_ref.shape[1] // LANES
    kw = qq_ref.shape[1] // n_pairs
    n_blocks = s_len // TQ
    lane_q = lax.broadcasted_iota(jnp.int32, (TQ, kw), 1)
    for pr in range(n_pairs):
        vt_scr[pr] = v_ref[:, pr * LANES:(pr + 1) * LANES].T

    q_masks = []
    for j in range(HEADS_PER_BLOCK):
        m = (lane_q >= j * HEAD_DIM) & (lane_q < (j + 1) * HEAD_DIM)
        if kw > LANES:
            m = m | ((lane_q >= LANES) & (lane_q % AUG_STRIDE == j))
        q_masks.append(m)
    p_scr = (p_scr0, p_scr1)
    units = [(pr, i) for pr in range(n_pairs) for i in range(n_blocks)]
    col_sum = [None] * len(units)

    def chunk(n, c, qcat):
        pr, i = units[n]
        keys = kk_ref[c * TQ:(c + 1) * TQ, pr * kw:(pr + 1) * kw]
        if qt_scr:
            s = jnp.dot(keys, qcat[...], preferred_element_type=F32)
        else:
            s = lax.dot_general(keys, qcat, (((1,), (1,)), ((), ())),
                                preferred_element_type=F32)
        if dense_bias or c == i:
            r0 = s_len - (i + 1 - c) * TQ
            g = g_ref[r0:r0 + TQ, :]
            s = s + jnp.concatenate([g] * HEADS_PER_BLOCK, axis=1)
        p = jnp.exp2(s)
        part = jnp.sum(p.reshape(TQ // SUBLANES, SUBLANES, p.shape[1]), axis=0)
        col_sum[n] = part if col_sum[n] is None else col_sum[n] + part
        p_scr[n % len(p_scr)][c * TQ:(c + 1) * TQ, :] = p.astype(BF16)

    def unit_steps(n):
        pr, i = units[n]
        qb = qq_ref[i * TQ:(i + 1) * TQ, pr * kw:(pr + 1) * kw]
        qcat = jnp.concatenate([jnp.where(q_masks[j], qb, jnp.zeros((), BF16))
                                for j in range(HEADS_PER_BLOCK)], axis=0)
        if qt_scr:
            qcat_t = qt_scr[n % len(qt_scr)]
            qcat_t[...] = qcat.T
            qcat = qcat_t
        return [functools.partial(chunk, n, c, qcat) for c in range(i + 1)]

    def finish(n):
        pr, i = units[n]
        q1 = (i + 1) * TQ
        o = jnp.dot(vt_scr[pr, :, 0:q1], p_scr[n % len(p_scr)][0:q1, :],
                    preferred_element_type=F32)
        sums = jnp.sum(col_sum[n], axis=0, keepdims=True)
        col_sum[n] = None
        res = []
        for j in range(HEADS_PER_BLOCK):
            num = o[j * HEAD_DIM:(j + 1) * HEAD_DIM, j * TQ:(j + 1) * TQ]
            res.append(num / sums[:, j * TQ:(j + 1) * TQ])
        o_ref[i * TQ:(i + 1) * TQ, pr * LANES:(pr + 1) * LANES] = (
            jnp.concatenate(res, axis=0).T.astype(o_ref.dtype))

    for n in range(len(units)):
        steps = unit_steps(n)
        pv_at = min(PV_LAG, len(steps) - 1)
        for k, step in enumerate(steps):
            step()
            if n > 0 and k == pv_at:
                finish(n - 1)
    finish(len(units) - 1)


def _staggered(stage_generators):
    waiting, active, done = list(stage_generators), [], object()
    while waiting or active:
        if waiting:
            active.append(waiting.pop(0))
        active = [g for g in active if next(g, done) is not done]


def _ffn_kernel(x_ref, of_ref, od_ref, wof_ref, wod_ref, wg_ref, wu_ref, wd_ref, out_ref):
    def norm(v):
        return v * lax.rsqrt(jnp.mean(v * v, axis=-1, keepdims=True) + EPS)

    rows = x_ref.shape[0] // FFN_SPLIT

    def row_group(rs):
        nf = norm(of_ref[rs, :].astype(F32)).astype(BF16)
        nd = norm(od_ref[rs, :].astype(F32)).astype(BF16)
        x1 = (x_ref[rs, :] + jnp.dot(nf, wof_ref[...], preferred_element_type=F32)
              + jnp.dot(nd, wod_ref[...], preferred_element_type=F32))
        yield
        h = norm(x1).astype(BF16)
        a = jnp.dot(h, wg_ref[...], preferred_element_type=F32)
        u = jnp.dot(h, wu_ref[...], preferred_element_type=F32)
        yield
        g = (a * (1.0 / (1.0 + jnp.exp(-a))) * u).astype(BF16)
        out_ref[rs, :] = x1 + jnp.dot(g, wd_ref[...], preferred_element_type=F32)

    _staggered(row_group(slice(r * rows, (r + 1) * rows)) for r in range(FFN_SPLIT))


def _rope_tables(s_len):
    half = ROPE_DIM // 2
    inv_freq = jnp.power(jnp.float32(ROPE_THETA),
                         -jnp.arange(half, dtype=jnp.float32) * 2.0 / ROPE_DIM)
    ang = jnp.arange(s_len).astype(jnp.float32)[:, None] * inv_freq[None, :]
    cos, sin = jnp.cos(ang), jnp.sin(ang)
    ones = jnp.ones((s_len, HEAD_DIM - ROPE_DIM), F32)
    zeros = jnp.zeros((s_len, HEAD_DIM - ROPE_DIM), F32)
    zh = jnp.zeros((s_len, half), F32)
    c_head = jnp.concatenate([cos, cos, ones], axis=1)
    s1_head = jnp.concatenate([-sin, zh, zeros], axis=1)
    s2_head = jnp.concatenate([zh, sin, zeros], axis=1)
    rep = lambda a: jnp.tile(a, (1, HEADS_PER_BLOCK))
    return rep(c_head), rep(s1_head), rep(s2_head)


def _bias_tables(s_len):
    i = np.arange(TQ)[None, :]
    jp = np.arange(s_len)[:, None]
    delta = i - jp + s_len - TQ
    causal = np.where(delta >= 0, 0.0, NEG).astype(np.float32)
    mult = np.zeros(delta.shape, np.float64)
    for window, dil in DILATION_PAIRS:
        mult += (delta >= 0) & (delta <= window) & (delta % dil == 0)
    dil_bias = np.where(mult > 0, np.log2(np.maximum(mult, 1.0)), NEG).astype(np.float32)
    return causal, dil_bias, float(np.log2(len(DILATION_PAIRS)))


def _resident(shape):
    nd = len(shape)
    return pl.BlockSpec(shape, lambda *_: (0,) * nd, pipeline_mode=pl.Buffered(1))


def _layer(x, g_mix, w_in, b_forget, g_q_fox, g_k_fox, g_q_dil, g_k_dil,
           g_out_fox, g_out_dil, w_out, g_ffn, w_gate, w_up, w_down):
    b_sz, s_len, d = x.shape
    n_heads = b_forget.shape[0]
    w = n_heads * HEAD_DIM
    d_ff = w_gate.shape[1]
    assert w_in.shape[1] == 6 * w + n_heads and w % MXU_DIM == 0
    assert s_len % TM_PROJ == 0 and s_len % TQ == 0 and (b_sz * s_len) % TM_FFN == 0
    assert d % LANES == 0 and n_heads <= AUG_STRIDE

    cols = np.cumsum([0, w, w, w, n_heads, w, w, w])
    w_in = g_mix[:, None] * w_in
    w_fox = w_in[:, cols[0]:cols[3]].astype(BF16)
    w_dil = w_in[:, cols[4]:cols[7]].astype(BF16)
    w_fa = jnp.zeros((d, LANES), F32).at[:, :n_heads].set(w_in[:, cols[3]:cols[4]]).astype(BF16)
    b_pad = jnp.zeros((1, LANES), F32).at[0, :n_heads].set(b_forget)
    scale = HEAD_DIM ** -0.5 * LOG2E
    gains = jnp.stack([jnp.tile(g_q_fox * scale, n_heads), jnp.tile(g_k_fox, n_heads),
                       jnp.tile(g_q_dil * scale, n_heads), jnp.tile(g_k_dil, n_heads)])
    head_id = np.arange(MXU_DIM) // HEAD_DIM
    bd = jnp.asarray((head_id[:, None] == head_id[None, :]) / HEAD_DIM, BF16)
    cos_t, s1_t, s2_t = _rope_tables(s_len)
    causal, dil_bias, dil_bias_max = _bias_tables(s_len)

    def logit_bound(gq, gk):
        return (HEAD_DIM * scale * BOUND_SLACK) * jnp.max(jnp.abs(gq)) * jnp.max(jnp.abs(gk))
    bound_fox = logit_bound(g_q_fox, g_k_fox)
    bound_dil = logit_bound(g_q_dil, g_k_dil)
    bounded_ok = 2.0 * jnp.maximum(bound_fox, bound_dil) + dil_bias_max <= EXP2_RANGE
    bnd_pad = jnp.zeros((1, LANES), F32).at[0, :n_heads].set(bound_fox)

    tok = lambda width: pl.BlockSpec((None, TM_PROJ, width), lambda b, t: (b, t, 0))
    pos = pl.BlockSpec((TM_PROJ, LANES), lambda b, t: (t, 0))
    bf = lambda width: jax.ShapeDtypeStruct((b_sz, s_len, width), BF16)
    qqf, kkf, vf, qd, kd, vd = pl.pallas_call(
        functools.partial(_proj_kernel, n_heads=n_heads),
        grid=(b_sz, s_len // TM_PROJ),
        in_specs=[tok(d), _resident((d, 3 * w)), _resident((d, 3 * w)),
                  _resident((d, LANES)),
                  _resident((1, LANES)), _resident((1, LANES)), _resident((4, w)),
                  _resident((MXU_DIM, MXU_DIM)), pos, pos, pos],
        out_specs=[tok(2 * w), tok(2 * w), tok(w), tok(w), tok(w), tok(w)],
        out_shape=[bf(2 * w), bf(2 * w), bf(w), bf(w), bf(w), bf(w)],
        scratch_shapes=[pltpu.VMEM((8, LANES), F32)],
        compiler_params=pltpu.CompilerParams(
            dimension_semantics=("arbitrary", "arbitrary"), vmem_limit_bytes=VMEM_LIMIT),
        name="proj",
    )(x, w_fox, w_dil, w_fa, b_pad, bnd_pad, gains, bd, cos_t, s1_t, s2_t)

    def attention(body, scratch, pairs, qq, kk, v, bias, dense_bias, name):
        kw = qq.shape[-1] // (w // LANES) * pairs
        blk = lambda width: pl.BlockSpec((None, s_len, width), lambda b, p: (b, 0, p))
        return pl.pallas_call(
            functools.partial(body, dense_bias=dense_bias),
            grid=(b_sz, w // LANES // pairs),
            in_specs=[blk(kw), blk(kw), blk(pairs * LANES), _resident((s_len, TQ))],
            out_specs=blk(pairs * LANES),
            out_shape=bf(w),
            scratch_shapes=scratch,
            compiler_params=pltpu.CompilerParams(
                dimension_semantics=("arbitrary", "arbitrary"), vmem_limit_bytes=VMEM_LIMIT),
            name=name,
        )(qq, kk, v, bias)

    nq = HEADS_PER_BLOCK * TQ
    vt_rows = LANES + ONES_ROWS
    exact_scratch = [pltpu.VMEM((vt_rows, s_len), BF16)] + [pltpu.VMEM((s_len, nq), F32)] * 3 + [
        pltpu.VMEM((vt_rows, nq), F32)]
    bounded_scratch = [pltpu.VMEM((PAIRS_PER_STEP, LANES, s_len), BF16)] + [
        pltpu.VMEM((s_len, nq), BF16)] * 2

    fox_scratch = bounded_scratch + [pltpu.VMEM((2 * LANES, nq), BF16)] * 2

    def attend_bounded():
        return (attention(_attn_bounded_kernel, fox_scratch, PAIRS_PER_STEP, qqf, kkf, vf,
                          jnp.asarray(causal), False, "attn_fox"),
                attention(_attn_bounded_kernel, bounded_scratch, PAIRS_PER_STEP, qd, kd, vd,
                          jnp.asarray(dil_bias) - (bound_dil + dil_bias_max), True, "attn_dil"))

    def attend_exact():
        return (attention(_attn_exact_kernel, exact_scratch, 1, qqf, kkf, vf,
                          jnp.asarray(causal), False, "attn_fox_exact"),
                attention(_attn_exact_kernel, exact_scratch, 1, qd, kd, vd,
                          jnp.asarray(dil_bias), True, "attn_dil_exact"))

    o_fox, o_dil = lax.cond(bounded_ok, attend_bounded, attend_exact)

    n_tok = b_sz * s_len
    rows = lambda width: pl.BlockSpec((TM_FFN, width), lambda t: (t, 0))
    out = pl.pallas_call(
        _ffn_kernel,
        grid=(n_tok // TM_FFN,),
        in_specs=[rows(d), rows(w), rows(w), _resident((w, d)), _resident((w, d)),
                  _resident((d, d_ff)), _resident((d, d_ff)), _resident((d_ff, d))],
        out_specs=rows(d),
        out_shape=jax.ShapeDtypeStruct((n_tok, d), x.dtype),
        compiler_params=pltpu.CompilerParams(
            dimension_semantics=("arbitrary",), vmem_limit_bytes=VMEM_LIMIT),
        name="ffn",
    )(x.reshape(n_tok, d), o_fox.reshape(n_tok, w), o_dil.reshape(n_tok, w),
      (g_out_fox[:, None] * w_out[:w]).astype(BF16), (g_out_dil[:, None] * w_out[w:]).astype(BF16),
      (g_ffn[:, None] * w_gate).astype(BF16), (g_ffn[:, None] * w_up).astype(BF16),
      w_down.astype(BF16))
    return out.reshape(b_sz, s_len, d)


def kernel(x, g_mix, w_in, b_forget, g_q_fox, g_k_fox, g_q_dil, g_k_dil, g_out_fox, g_out_dil,
           w_out, g_ffn, w_gate, w_up, w_down):
    for l in range(g_mix.shape[0]):
        x = _layer(x, g_mix[l], w_in[l], b_forget[l], g_q_fox[l], g_k_fox[l], g_q_dil[l],
                   g_k_dil[l], g_out_fox[l], g_out_dil[l], w_out[l], g_ffn[l], w_gate[l],
                   w_up[l], w_down[l])
    return x
```
